```python
import math
import jax, jax.numpy as jnp
from jax import lax
import numpy as np

D_MODEL = 1024
BATCH = 4
SEQ = 8192
DEPTH = 4

GRID_W = 64
CTX_LEN = 256
HEAD_DIM = 64
ROPE_THETA = 10000.0
Q_BLOCK = 128
NORM_EPS = 1e-6
CONV_CH = D_MODEL // 2
CONV_WIDTH = 31
GQA_Q_HEADS = 8
GQA_KV_HEADS = 2
DIFF_HEADS = 4
DIFF_V_DIM = 2 * HEAD_DIM
N_BRANCH = 3
N_EXPERTS = 32
TOP_K = 4
EXPERT_FF = D_MODEL
SWIGLU_ALPHA = 1.702
SWIGLU_LIMIT = 7.0
EXPERT_BLOCK = 256

GQA_Q = GQA_Q_HEADS * HEAD_DIM
GQA_KV = GQA_KV_HEADS * HEAD_DIM
DIFF_QK = DIFF_HEADS * 2 * HEAD_DIM
DIFF_V = DIFF_HEADS * DIFF_V_DIM
KV_COLS = 2 * GQA_KV + DIFF_QK + DIFF_V
Q_COLS = GQA_Q + DIFF_QK
CONV_COLS = 2 * CONV_CH
GATE_COLS = N_BRANCH * D_MODEL
D_IN = KV_COLS + Q_COLS + CONV_COLS + GATE_COLS
KV_SPLITS = (GQA_KV, 2 * GQA_KV, 2 * GQA_KV + DIFF_QK)

kernel_name = "hybrid_conv_gqa_diffattn_moe_dit"


def _rmsnorm(x, g):
    xf = x.astype(jnp.float32)
    y = xf * lax.rsqrt(jnp.mean(xf * xf, axis=-1, keepdims=True) + NORM_EPS)
    return (y * g.astype(jnp.float32)).astype(x.dtype)


def _layernorm(x, g, b):
    xf = x.astype(jnp.float32)
    mu = jnp.mean(xf, axis=-1, keepdims=True)
    xc = xf - mu
    var = jnp.mean(xc * xc, axis=-1, keepdims=True)
    return (xc * lax.rsqrt(var + NORM_EPS) * g.astype(jnp.float32) + b.astype(jnp.float32)).astype(x.dtype)


def _modulated_norm(x, g, shift, scale):
    return _rmsnorm(x, g) * (1 + scale) + shift


def _axial_rope_tables(n_tok, dtype):
    rows = n_tok // GRID_W
    t_row = jnp.repeat(jnp.arange(rows, dtype=jnp.float32), GRID_W)
    t_col = jnp.tile(jnp.arange(GRID_W, dtype=jnp.float32), rows)
    n_freq = HEAD_DIM // 4
    inv_freq = ROPE_THETA ** (-jnp.arange(n_freq, dtype=jnp.float32) / n_freq)
    ang_r = t_row[:, None] * inv_freq[None, :]
    ang_c = t_col[:, None] * inv_freq[None, :]
    return tuple(a[:, None, :].astype(dtype) for a in (jnp.cos(ang_r), jnp.sin(ang_r), jnp.cos(ang_c), jnp.sin(ang_c)))


def _rot_half(x, cos, sin):
    x1, x2 = jnp.split(x, 2, axis=-1)
    return jnp.concatenate([x1 * cos - x2 * sin, x2 * cos + x1 * sin], axis=-1)


def _axial_rope(x, rope):
    cr, sr, cc, sc = rope
    xr, xc = jnp.split(x, 2, axis=-1)
    return jnp.concatenate([_rot_half(xr, cr, sr), _rot_half(xc, cc, sc)], axis=-1)


def _sweep_query_blocks(fn, q):
    b, n = q.shape[:2]
    nb = n // Q_BLOCK
    qb = jnp.moveaxis(q.reshape(b, nb, Q_BLOCK, *q.shape[2:]), 1, 0)
    out = lax.map(fn, qb)
    return jnp.moveaxis(out, 0, 1).reshape(b, n, *out.shape[3:])


def _gqa(q, k, v):
    b, nq = q.shape[:2]
    qg = q.reshape(b, nq, GQA_KV_HEADS, GQA_Q_HEADS // GQA_KV_HEADS, HEAD_DIM)
    s = jnp.einsum('bqhgd,bkhd->bhgqk', qg, k).astype(jnp.float32) * (HEAD_DIM ** -0.5)
    p = jax.nn.softmax(s, axis=-1).astype(v.dtype)
    o = jnp.einsum('bhgqk,bkhd->bqhgd', p, v)
    return o.reshape(b, nq, GQA_Q)


def _diff_attn(q, k, v, lam):
    s = jnp.einsum('bqhcd,bkhcd->bhcqk', q, k).astype(jnp.float32) * (HEAD_DIM ** -0.5)
    p = jax.nn.softmax(s, axis=-1)
    a = (p[:, :, 0] - lam * p[:, :, 1]).astype(v.dtype)
    return jnp.einsum('bhqk,bkhe->bqhe', a, v)


def _keys_values(z_kv, rope, gk_g, dk_g):
    b, n = z_kv.shape[:2]
    gk, gv, dk, dv = jnp.split(z_kv, KV_SPLITS, axis=-1)
    gk = _rmsnorm(gk.reshape(b, n, GQA_KV_HEADS, HEAD_DIM), gk_g)
    dk = _rmsnorm(dk.reshape(b, n, 2 * DIFF_HEADS, HEAD_DIM), dk_g)
    if rope is not None:
        gk = _axial_rope(gk, rope)
        dk = _axial_rope(dk, rope)
    return (gk, gv.reshape(b, n, GQA_KV_HEADS, HEAD_DIM),
            dk.reshape(b, n, DIFF_HEADS, 2, HEAD_DIM), dv.reshape(b, n, DIFF_HEADS, DIFF_V_DIM))


def _conformer_conv(a, dw_w, dw_b, ln_g, ln_b, pw_w):
    u = a[..., :CONV_CH] * jax.nn.sigmoid(a[..., CONV_CH:])
    u = lax.conv_general_dilated(u, dw_w[:, None, :], window_strides=(1,),
                                 padding=((CONV_WIDTH // 2, CONV_WIDTH // 2),),
                                 dimension_numbers=('NWC', 'WIO', 'NWC'),
                                 feature_group_count=CONV_CH) + dw_b
    u = jax.nn.silu(_layernorm(u, ln_g, ln_b))
    return u @ pw_w


def _mixer_output(z, kv, rope, lam, lam_init, gq_g, dq_g, subln_g, dw_w, dw_b, ln_g, ln_b, pw_w,
                  w_gqa_o, w_diff_o, b_gate, w_out):
    b, n = z.shape[:2]
    gk, gv, dk, dv = kv
    zq = z[..., KV_COLS:KV_COLS + Q_COLS]
    a = z[..., KV_COLS + Q_COLS:KV_COLS + Q_COLS + CONV_COLS]
    gate_pre = z[..., KV_COLS + Q_COLS + CONV_COLS:]
    gq = _rmsnorm(zq[..., :GQA_Q].reshape(b, n, GQA_Q_HEADS, HEAD_DIM), gq_g)
    dq = _rmsnorm(zq[..., GQA_Q:].reshape(b, n, 2 * DIFF_HEADS, HEAD_DIM), dq_g)
    if rope is not None:
        gq = _axial_rope(gq, rope)
        dq = _axial_rope(dq, rope)
    dq = dq.reshape(b, n, DIFF_HEADS, 2, HEAD_DIM)
    o_g = _sweep_query_blocks(lambda qb: _gqa(qb, gk, gv), gq)
    o_d = _sweep_query_blocks(lambda qb: _diff_attn(qb, dk, dv, lam), dq)
    o_d = (_rmsnorm(o_d, subln_g) * (1.0 - lam_init)).reshape(b, n, DIFF_V)
    br_a = _conformer_conv(a, dw_w, dw_b, ln_g, ln_b, pw_w)
    br_b = o_g @ w_gqa_o
    br_c = o_d @ w_diff_o
    g = jax.nn.sigmoid(gate_pre + b_gate).reshape(b, n, N_BRANCH, D_MODEL)
    merged = g[:, :, 0] * br_a + g[:, :, 1] * br_b + g[:, :, 2] * br_c
    return merged @ w_out


def _moe(h, w_r, b_r, w1, b1, w2, b2):
    t, d = h.shape
    logits = (h @ w_r).astype(jnp.float32) + b_r.astype(jnp.float32)
    top_val, top_idx = lax.top_k(logits, TOP_K)
    comb = jax.nn.softmax(top_val, axis=-1)
    flat_e = top_idx.reshape(-1)
    order = jnp.argsort(flat_e)
    e_sorted = flat_e[order]
    tok_sorted = order // TOP_K
    w_sorted = comb.reshape(-1)[order]
    counts = jnp.bincount(flat_e, length=N_EXPERTS)
    padded = (counts + EXPERT_BLOCK - 1) // EXPERT_BLOCK * EXPERT_BLOCK
    pad_end = jnp.cumsum(padded)
    pad_start = pad_end - padded
    start = jnp.cumsum(counts) - counts
    n_assign = t * TOP_K
    slot = pad_start[e_sorted] + jnp.arange(n_assign) - start[e_sorted]
    n_blocks = -(-n_assign // EXPERT_BLOCK) + N_EXPERTS
    n_slots = n_blocks * EXPERT_BLOCK
    slot_tok = jnp.full((n_slots,), t, dtype=tok_sorted.dtype).at[slot].set(tok_sorted)
    h_pad = jnp.concatenate([h, jnp.zeros((1, d), h.dtype)], axis=0)
    xb = h_pad[slot_tok].reshape(n_blocks, EXPERT_BLOCK, d)
    blk_e = jnp.minimum(jnp.searchsorted(pad_end, jnp.arange(n_blocks) * EXPERT_BLOCK, side='right'), N_EXPERTS - 1)

    def expert_block(args):
        xe, e = args
        gu = xe @ w1[e] + b1[e]
        gate = jnp.minimum(gu[:, :EXPERT_FF], SWIGLU_LIMIT)
        lin = jnp.clip(gu[:, EXPERT_FF:], -SWIGLU_LIMIT, SWIGLU_LIMIT)
        act = (lin + 1) * (gate * jax.nn.sigmoid(gate * SWIGLU_ALPHA))
        return act @ w2[e] + b2[e]

    yb = lax.map(expert_block, (xb, blk_e)).reshape(n_slots, d)
    y_assign = yb[slot] * w_sorted[:, None].astype(yb.dtype)
    return jax.ops.segment_sum(y_assign, tok_sorted, num_segments=t)


def setup_inputs(seed: int = 0) -> dict:
    key = jax.random.key(seed)
    ks = iter(jax.random.split(key, 40))
    L, D, E, F = DEPTH, D_MODEL, N_EXPERTS, EXPERT_FF

    def nrm(shape, scale):
        return jax.random.normal(next(ks), shape, jnp.float32) * scale

    def gain(shape):
        return 1.0 + nrm(shape, 0.02)

    return {
        "x": nrm((BATCH, SEQ, D), 1.0),
        "c": nrm((BATCH, D), 1.0),
        "ctx": nrm((BATCH, CTX_LEN, D), 1.0),
        "c_ctx": nrm((D,), 1.0),
        "w_mod": nrm((L, D, 6 * D), 0.5 * D ** -0.5),
        "b_mod": nrm((L, 6 * D), 0.02),
        "norm1_g": gain((L, D)),
        "norm2_g": gain((L, D)),
        "w_in": nrm((L, D, D_IN), D ** -0.5),
        "gqa_q_norm": gain((L, HEAD_DIM)),
        "gqa_k_norm": gain((L, HEAD_DIM)),
        "diff_q_norm": gain((L, HEAD_DIM)),
        "diff_k_norm": gain((L, HEAD_DIM)),
        "lam_q1": nrm((L, HEAD_DIM), 0.1),
        "lam_k1": nrm((L, HEAD_DIM), 0.1),
        "lam_q2": nrm((L, HEAD_DIM), 0.1),
        "lam_k2": nrm((L, HEAD_DIM), 0.1),
        "diff_subln_g": gain((L, DIFF_V_DIM)),
        "conv_dw_w": nrm((L, CONV_WIDTH, CONV_CH), CONV_WIDTH ** -0.5),
        "conv_dw_b": nrm((L, CONV_CH), 0.02),
        "conv_ln_g": gain((L, CONV_CH)),
        "conv_ln_b": nrm((L, CONV_CH), 0.02),
        "conv_pw_w": nrm((L, CONV_CH, D), CONV_CH ** -0.5),
        "w_gqa_o": nrm((L, GQA_Q, D), GQA_Q ** -0.5),
        "w_diff_o": nrm((L, DIFF_V, D), DIFF_V ** -0.5),
        "b_gate": nrm((L, GATE_COLS), 0.02),
        "w_out": nrm((L, D, D), D ** -0.5),
        "router_w": nrm((L, D, E), D ** -0.5),
        "router_b": nrm((L, E), 0.01),
        "exp_w1": nrm((L, E, D, 2 * F), D ** -0.5),
        "exp_b1": nrm((L, E, 2 * F), 0.02),
        "exp_w2": nrm((L, E, F, D), F ** -0.5),
        "exp_b2": nrm((L, E, D), 0.02),
    }


def reference(x, c, ctx, c_ctx, w_mod, b_mod, norm1_g, norm2_g, w_in, gqa_q_norm, gqa_k_norm,
              diff_q_norm, diff_k_norm, lam_q1, lam_k1, lam_q2, lam_k2, diff_subln_g, conv_dw_w, conv_dw_b,
              conv_ln_g, conv_ln_b, conv_pw_w, w_gqa_o, w_diff_o, b_gate, w_out, router_w, router_b,
              exp_w1, exp_b1, exp_w2, exp_b2):
    b, n_lat, d = x.shape
    n_ctx = ctx.shape[1]
    rope = _axial_rope_tables(n_lat, x.dtype)
    for l in range(DEPTH):
        last = l == DEPTH - 1
        lam_init = 0.8 - 0.6 * math.exp(-0.3 * l)
        lam = (jnp.exp(jnp.sum(lam_q1[l].astype(jnp.float32) * lam_k1[l].astype(jnp.float32)))
               - jnp.exp(jnp.sum(lam_q2[l].astype(jnp.float32) * lam_k2[l].astype(jnp.float32))) + lam_init)
        mod = jax.nn.silu(c) @ w_mod[l] + b_mod[l]
        mod_c = jax.nn.silu(c_ctx) @ w_mod[l] + b_mod[l]
        sh1, sc1, g1, sh2, sc2, g2 = jnp.split(mod[:, None, :], 6, axis=-1)
        csh1, csc1, cg1, csh2, csc2, cg2 = jnp.split(mod_c, 6, axis=-1)
        branch_params = (diff_subln_g[l], conv_dw_w[l], conv_dw_b[l], conv_ln_g[l], conv_ln_b[l], conv_pw_w[l],
                         w_gqa_o[l], w_diff_o[l], b_gate[l], w_out[l])

        h = _modulated_norm(x, norm1_g[l], sh1, sc1)
        hc = _modulated_norm(ctx, norm1_g[l], csh1, csc1)
        zc = hc @ (w_in[l][:, :KV_COLS] if last else w_in[l])
        kv_c = _keys_values(zc[..., :KV_COLS], None, gqa_k_norm[l], diff_k_norm[l])
        z = h @ w_in[l]
        kv_l = _keys_values(z[..., :KV_COLS], rope, gqa_k_norm[l], diff_k_norm[l])
        kv_all = tuple(jnp.concatenate([u, v], axis=1) for u, v in zip(kv_c, kv_l))
        y = _mixer_output(z, kv_all, rope, lam, lam_init, gqa_q_norm[l], diff_q_norm[l], *branch_params)
        if not last:
            yc = _mixer_output(zc, kv_c, None, lam, lam_init, gqa_q_norm[l], diff_q_norm[l], *branch_params)
            ctx = ctx + cg1 * yc
        x = x + g1 * y

        h2 = _modulated_norm(x, norm2_g[l], sh2, sc2).reshape(b * n_lat, d)
        moe_args = (router_w[l], router_b[l], exp_w1[l], exp_b1[l], exp_w2[l], exp_b2[l])
        if last:
            f = _moe(h2, *moe_args)
            x = x + g2 * f.reshape(b, n_lat, d)
        else:
            h2c = _modulated_norm(ctx, norm2_g[l], csh2, csc2).reshape(b * n_ctx, d)
            f = _moe(jnp.concatenate([h2, h2c], axis=0), *moe_args)
            x = x + g2 * f[:b * n_lat].reshape(b, n_lat, d)
            ctx = ctx + cg2 * f[b * n_lat:].reshape(b, n_ctx, d)
    return x
```

```python
import functools
import math

import jax
import jax.numpy as jnp
from jax import lax
from jax.experimental import pallas as pl
from jax.experimental.pallas import tpu as pltpu

F32 = jnp.float32
BF16 = jnp.bfloat16

D = 1024
HD = 64
GRID_W = 64
ROPE_THETA = 10000.0
EPS = 1e-6
CONV_CH = 512
CONV_W = 31
CONV_HALO = 16
GQA_QH = 8
GQA_KVH = 2
DIFF_H = 4
DIFF_V = 128
N_EXP = 32
TOP_K = 4
FF = 1024
SWIGLU_ALPHA = 1.702
SWIGLU_LIMIT = 7.0
EXPERT_BLOCK = 256
TM = 256
LANES = 128

N_KHEADS = GQA_KVH + 2 * DIFF_H
N_QHEADS = GQA_QH + 2 * DIFF_H
QK_COLS = (N_KHEADS + N_QHEADS) * HD
V_COLS = GQA_KVH * HD + DIFF_H * DIFF_V
GATE_COLS = 3 * D
D_IN = QK_COLS + V_COLS + 2 * CONV_CH + GATE_COLS
VMEM_LIMIT = 56 * 1024 * 1024


def _cparams(sem):
    return pltpu.CompilerParams(dimension_semantics=sem, vmem_limit_bytes=VMEM_LIMIT)


def _mod_kernel(c_ref, w_ref, b_ref, o_ref):
    c = c_ref[...]
    s = c * (1.0 / (1.0 + jnp.exp(-c)))
    o_ref[0] = jnp.dot(s, w_ref[0], preferred_element_type=F32,
                       precision=lax.Precision.HIGHEST) + b_ref[0]


def _modulation(cc, w_mod, b_mod):
    L = w_mod.shape[0]
    tn = 1536
    return pl.pallas_call(
        _mod_kernel,
        out_shape=jax.ShapeDtypeStruct((L, 8, 6 * D), F32),
        grid=(L, 6 * D // tn),
        in_specs=[pl.BlockSpec((8, D), lambda l, j: (0, 0)),
                  pl.BlockSpec((1, D, tn), lambda l, j: (l, 0, j)),
                  pl.BlockSpec((1, 1, tn), lambda l, j: (l, 0, j))],
        out_specs=pl.BlockSpec((1, 8, tn), lambda l, j: (l, 0, j)),
        compiler_params=_cparams(("arbitrary", "arbitrary")),
    )(cc, w_mod, b_mod.reshape(L, 1, 6 * D))


def _modnorm(x, g, shift, scale):
    ms = jnp.mean(x * x, axis=-1, keepdims=True)
    return (x * lax.rsqrt(ms + EPS) * g) * (1.0 + scale) + shift


def _inproj_kernel(x_ref, mod_ref, g1_ref, w_ref, cos_ref, sin_ref, qkg_ref, bg_ref,
                   k_out, qt_out, vt_out, u_out, g_out):
    h = _modnorm(x_ref[...], g1_ref[...], mod_ref[0, 0:1, :], mod_ref[0, 1:2, :]).astype(BF16)
    cos = cos_ref[...]
    sin = sin_ref[...]
    lane = lax.broadcasted_iota(jnp.int32, (1, LANES), 1)
    first = (lane % 32) < 16
    r_i = lax.broadcasted_iota(jnp.int32, (LANES, LANES), 0) // HD
    c_i = lax.broadcasted_iota(jnp.int32, (LANES, LANES), 1) // HD
    head_ones = jnp.where(r_i == c_i, 1.0, 0.0).astype(BF16)

    def proj(c0, c1):
        return jnp.dot(h, w_ref[:, c0:c1], preferred_element_type=F32)

    n_pairs = QK_COLS // LANES
    pairs_per_dot = 4
    for p0 in range(0, n_pairs, pairs_per_dot):
        p1 = min(p0 + pairs_per_dot, n_pairs)
        z = proj(p0 * LANES, p1 * LANES)
        for p in range(p0, p1):
            zc = z[:, (p - p0) * LANES:(p - p0 + 1) * LANES]
            ss = jnp.dot((zc * zc).astype(BF16), head_ones, preferred_element_type=F32)
            n = zc * lax.rsqrt(ss * (1.0 / HD) + EPS) * qkg_ref[:, p * LANES:(p + 1) * LANES]
            partner = jnp.where(first, pltpu.roll(n, LANES - 16, 1), pltpu.roll(n, 16, 1))
            r = n * cos + partner * sin
            if 2 * p < N_KHEADS:
                k_out[2 * p] = r[:, :HD].astype(BF16)
                k_out[2 * p + 1] = r[:, HD:].astype(BF16)
            else:
                rt = r.T
                qh = 2 * p - N_KHEADS
                qt_out[qh] = rt[:HD, :].astype(BF16)
                qt_out[qh + 1] = rt[HD:, :].astype(BF16)

    z = proj(QK_COLS, QK_COLS + V_COLS)
    for p in range(V_COLS // LANES):
        vt_out[0, p * LANES:(p + 1) * LANES, :] = z[:, p * LANES:(p + 1) * LANES].T.astype(BF16)

    c0 = QK_COLS + V_COLS
    a = proj(c0, c0 + 2 * CONV_CH)
    gl = a[:, CONV_CH:]
    u_out[...] = a[:, :CONV_CH] * (1.0 / (1.0 + jnp.exp(-gl)))

    c0 = c0 + 2 * CONV_CH
    for j in range(3):
        gp = proj(c0 + j * D, c0 + (j + 1) * D) + bg_ref[:, j * D:(j + 1) * D]
        g_out[:, j * D:(j + 1) * D] = (1.0 / (1.0 + jnp.exp(-gp))).astype(BF16)


def _mod_row(i, tiles_per_b, n_b):
    return jnp.where(i % tiles_per_b == 0, n_b, i // tiles_per_b)


def _inproj(x, mod_l, g1, w_perm, cos_t, sin_t, qk_gain, b_gate, tiles_per_b, n_b):
    T = x.shape[0]
    nt = T // TM
    out_shape = (
        jax.ShapeDtypeStruct((N_KHEADS, T, HD), BF16),
        jax.ShapeDtypeStruct((N_QHEADS, HD, T), BF16),
        jax.ShapeDtypeStruct((nt, V_COLS, TM), BF16),
        jax.ShapeDtypeStruct((T, CONV_CH), F32),
        jax.ShapeDtypeStruct((T, GATE_COLS), BF16),
    )
    in_specs = [
        pl.BlockSpec((TM, D), lambda i: (i, 0)),
        pl.BlockSpec((1, 6, D), lambda i: (_mod_row(i, tiles_per_b, n_b), 0, 0)),
        pl.BlockSpec((1, D), lambda i: (0, 0)),
        pl.BlockSpec((D, D_IN), lambda i: (0, 0)),
        pl.BlockSpec((TM, LANES), lambda i: (i % tiles_per_b, 0)),
        pl.BlockSpec((TM, LANES), lambda i: (i % tiles_per_b, 0)),
        pl.BlockSpec((1, QK_COLS), lambda i: (0, 0)),
        pl.BlockSpec((1, GATE_COLS), lambda i: (0, 0)),
    ]
    out_specs = (
        pl.BlockSpec((N_KHEADS, TM, HD), lambda i: (0, i, 0)),
        pl.BlockSpec((N_QHEADS, HD, TM), lambda i: (0, 0, i)),
        pl.BlockSpec((1, V_COLS, TM), lambda i: (i, 0, 0)),
        pl.BlockSpec((TM, CONV_CH), lambda i: (i, 0)),
        pl.BlockSpec((TM, GATE_COLS), lambda i: (i, 0)),
    )
    return pl.pallas_call(
        _inproj_kernel, out_shape=out_shape, grid=(nt,),
        in_specs=in_specs, out_specs=out_specs,
        compiler_params=_cparams(("arbitrary",)),
    )(x, mod_l, g1, w_perm, cos_t, sin_t, qk_gain, b_gate)


def _attn_kernel(*refs, n_maps, share_k, dv, diff, lam_init):
    if diff:
        (qt_ref, k_ref, vt_ref, lq1, lk1, lq2, lk2, sg_ref, o_ref, m_sc, l_sc, acc_sc) = refs
    else:
        (qt_ref, k_ref, vt_ref, o_ref, m_sc, l_sc, acc_sc) = refs
    qi = pl.program_id(2)
    n_chunks = jnp.where(qi == 0, 1, vt_ref.shape[0])

    m_sc[...] = jnp.full(m_sc.shape, -jnp.inf, F32)
    l_sc[...] = jnp.zeros(l_sc.shape, F32)
    acc_sc[...] = jnp.zeros(acc_sc.shape, F32)

    def body(j, carry):
        off = pl.multiple_of(j * TM, TM)
        vt = vt_ref[j]
        for m in range(n_maps):
            kc = k_ref[0 if share_k else m, pl.ds(off, TM), :]
            s = jnp.dot(kc, qt_ref[m], preferred_element_type=F32)
            m_prev = m_sc[m]
            m_new = jnp.maximum(m_prev, jnp.max(s, axis=0, keepdims=True))
            p = jnp.exp(s - m_new)
            alpha = jnp.exp(m_prev - m_new)
            l_sc[m] = alpha * l_sc[m] + jnp.sum(p, axis=0, keepdims=True)
            acc_sc[m] = alpha * acc_sc[m] + jnp.dot(vt, p.astype(BF16), preferred_element_type=F32)
            m_sc[m] = m_new
        return carry

    lax.fori_loop(0, n_chunks, body, 0)

    if diff:
        lam = (jnp.exp(jnp.sum(lq1[...] * lk1[...], axis=-1, keepdims=True))
               - jnp.exp(jnp.sum(lq2[...] * lk2[...], axis=-1, keepdims=True)) + lam_init)
        o = (acc_sc[0] / l_sc[0] - lam * (acc_sc[1] / l_sc[1])).T
        ms = jnp.mean(o * o, axis=-1, keepdims=True)
        o_ref[...] = (o * lax.rsqrt(ms + EPS) * sg_ref[...] * (1.0 - lam_init)).astype(o_ref.dtype)
    else:
        o = jnp.concatenate([acc_sc[m] / l_sc[m] for m in range(n_maps)], axis=0)
        o_ref[...] = o.T.astype(o_ref.dtype)


def _attention(qt, k3, vt3, n_b, tiles_per_b, *, diff, lam_refs=None, subln_g=None, lam_init=0.0):
    T = qt.shape[2]
    if diff:
        n_groups, n_maps, share_k, dv = DIFF_H, 2, False, DIFF_V
        q_unit0 = GQA_QH // 2
        k_unit0 = GQA_KVH // 2
        v_unit0 = GQA_KVH * HD // DIFF_V
        k_blk = 2
    else:
        n_groups, n_maps, share_k, dv = GQA_KVH, GQA_QH // GQA_KVH, True, HD
        q_unit0 = k_unit0 = v_unit0 = 0
        k_blk = 1
    vt4 = vt3.reshape(n_b, tiles_per_b, V_COLS, TM)
    tokens_per_b = tiles_per_b * TM
    in_specs = [
        pl.BlockSpec((n_maps, HD, TM), lambda b, g, q: (q_unit0 + g, 0, b * tiles_per_b + q)),
        pl.BlockSpec((k_blk, tokens_per_b, HD), lambda b, g, q: (k_unit0 + g, b, 0)),
        pl.BlockSpec((None, tiles_per_b, dv, TM), lambda b, g, q: (b, 0, v_unit0 + g, 0)),
    ]
    args = [qt, k3, vt4]
    if diff:
        in_specs += [pl.BlockSpec((1, HD), lambda b, g, q: (0, 0))] * 4
        in_specs += [pl.BlockSpec((1, DIFF_V), lambda b, g, q: (0, 0))]
        args += list(lam_refs) + [subln_g]
    out_cols = n_maps * dv if not diff else dv
    kern = functools.partial(_attn_kernel, n_maps=n_maps, share_k=share_k, dv=dv, diff=diff,
                             lam_init=lam_init)
    return pl.pallas_call(
        kern,
        out_shape=jax.ShapeDtypeStruct((T, n_groups * out_cols), BF16),
        grid=(n_b, n_groups, tiles_per_b),
        in_specs=in_specs,
        out_specs=pl.BlockSpec((TM, out_cols), lambda b, g, q: (b * tiles_per_b + q, g)),
        scratch_shapes=[pltpu.VMEM((n_maps, 1, TM), F32), pltpu.VMEM((n_maps, 1, TM), F32),
                        pltpu.VMEM((n_maps, dv, TM), F32)],
        compiler_params=_cparams(("arbitrary", "arbitrary", "arbitrary")),
    )(*args)


def _mix_kernel(up_ref, uc_ref, un_ref, og_ref, od_ref, gt_ref, x_ref, mod_ref,
                dww_ref, dwb_ref, lng_ref, lnb_ref, pw_ref, wgo_ref, wdo_ref, wout_ref,
                g2_ref, wr_ref, br_ref,
                xo_ref, h2_ref, lg_ref, ext_sc, *, tiles_per_b):
    i = pl.program_id(0)
    ti = i % tiles_per_b
    has_prev = ti > 1
    has_next = jnp.logical_and(ti > 0, ti < tiles_per_b - 1)
    ext_sc[0:CONV_HALO, :] = jnp.where(has_prev, up_ref[...], 0.0)
    ext_sc[CONV_HALO:CONV_HALO + TM, :] = uc_ref[...]
    ext_sc[CONV_HALO + TM:, :] = jnp.where(has_next, un_ref[...], 0.0)
    acc = jnp.zeros((TM, CONV_CH), F32) + dwb_ref[...]
    base = CONV_HALO - CONV_W // 2
    for w in range(CONV_W):
        acc = acc + ext_sc[base + w:base + w + TM, :] * dww_ref[w:w + 1, :]
    mu = jnp.mean(acc, axis=-1, keepdims=True)
    xc = acc - mu
    var = jnp.mean(xc * xc, axis=-1, keepdims=True)
    ln = xc * lax.rsqrt(var + EPS) * lng_ref[...] + lnb_ref[...]
    act = ln * (1.0 / (1.0 + jnp.exp(-ln)))
    br_a = jnp.dot(act.astype(BF16), pw_ref[...], preferred_element_type=F32)
    br_b = jnp.dot(og_ref[...], wgo_ref[...], preferred_element_type=F32)
    br_c = jnp.dot(od_ref[...], wdo_ref[...], preferred_element_type=F32)
    merged = (gt_ref[:, 0:D].astype(F32) * br_a + gt_ref[:, D:2 * D].astype(F32) * br_b
              + gt_ref[:, 2 * D:3 * D].astype(F32) * br_c)
    y = jnp.dot(merged.astype(BF16), wout_ref[...], preferred_element_type=F32)
    x_new = x_ref[...] + mod_ref[0, 2:3, :] * y
    xo_ref[...] = x_new
    h2 = _modnorm(x_new, g2_ref[...], mod_ref[0, 3:4, :], mod_ref[0, 4:5, :])
    h2_ref[...] = h2.astype(BF16)
    lg_ref[...] = jnp.dot(h2, wr_ref[...], preferred_element_type=F32,
                          precision=lax.Precision.HIGHEST) + br_ref[...]


def _mix(u, og, od, gates, x, mod_l, dw_w, dw_b, ln_g, ln_b, pw_w, w_gqa_o, w_diff_o, w_out,
         g2, w_r, b_r, tiles_per_b, n_b):
    T = x.shape[0]
    nt = T // TM
    halo_per_tile = TM // CONV_HALO
    n_halo = T // CONV_HALO
    full = lambda shape: pl.BlockSpec(shape, lambda i: (0,) * len(shape))
    in_specs = [
        pl.BlockSpec((CONV_HALO, CONV_CH), lambda i: (jnp.maximum(i * halo_per_tile - 1, 0), 0)),
        pl.BlockSpec((TM, CONV_CH), lambda i: (i, 0)),
        pl.BlockSpec((CONV_HALO, CONV_CH),
                     lambda i: (jnp.minimum((i + 1) * halo_per_tile, n_halo - 1), 0)),
        pl.BlockSpec((TM, GQA_QH * HD), lambda i: (i, 0)),
        pl.BlockSpec((TM, DIFF_H * DIFF_V), lambda i: (i, 0)),
        pl.BlockSpec((TM, GATE_COLS), lambda i: (i, 0)),
        pl.BlockSpec((TM, D), lambda i: (i, 0)),
        pl.BlockSpec((1, 6, D), lambda i: (_mod_row(i, tiles_per_b, n_b), 0, 0)),
        full((CONV_W + 1, CONV_CH)), full((1, CONV_CH)), full((1, CONV_CH)), full((1, CONV_CH)),
        full((CONV_CH, D)), full((GQA_QH * HD, D)), full((DIFF_H * DIFF_V, D)), full((D, D)),
        full((1, D)), full((D, N_EXP)), full((1, N_EXP)),
    ]
    out_shape = (jax.ShapeDtypeStruct((T, D), F32), jax.ShapeDtypeStruct((T, D), BF16),
                 jax.ShapeDtypeStruct((T, N_EXP), F32))
    out_specs = (pl.BlockSpec((TM, D), lambda i: (i, 0)), pl.BlockSpec((TM, D), lambda i: (i, 0)),
                 pl.BlockSpec((TM, N_EXP), lambda i: (i, 0)))
    return pl.pallas_call(
        functools.partial(_mix_kernel, tiles_per_b=tiles_per_b),
        out_shape=out_shape, grid=(nt,), in_specs=in_specs, out_specs=out_specs,
        scratch_shapes=[pltpu.VMEM((TM + 2 * CONV_HALO, CONV_CH), F32)],
        compiler_params=_cparams(("arbitrary",)),
    )(u, u, u, og, od, gates, x, mod_l, dw_w, dw_b, ln_g, ln_b, pw_w, w_gqa_o, w_diff_o, w_out,
      g2, w_r, b_r)


def _expert_kernel(be_ref, nb_ref, x_ref, w1_ref, b1_ref, w2_ref, b2_ref, y_ref, w1_sc, w2_sc):
    i = pl.program_id(0)
    prev_e = be_ref[jnp.maximum(i - 1, 0)]
    new_expert = jnp.logical_or(i == 0, be_ref[i] != prev_e)

    @pl.when(new_expert)
    def _():
        w1_sc[...] = w1_ref[0].astype(BF16)
        w2_sc[...] = w2_ref[0].astype(BF16)

    @pl.when(i < nb_ref[0])
    def _():
        gu = jnp.dot(x_ref[...], w1_sc[...], preferred_element_type=F32) + b1_ref[0]
        gate = jnp.minimum(gu[:, :FF], SWIGLU_LIMIT)
        lin = jnp.clip(gu[:, FF:], -SWIGLU_LIMIT, SWIGLU_LIMIT)
        act = (lin + 1.0) * (gate * (1.0 / (1.0 + jnp.exp(-gate * SWIGLU_ALPHA))))
        y = jnp.dot(act.astype(BF16), w2_sc[...], preferred_element_type=F32) + b2_ref[0]
        y_ref[...] = y.astype(y_ref.dtype)

    @pl.when(i >= nb_ref[0])
    def _():
        y_ref[...] = jnp.zeros(y_ref.shape, y_ref.dtype)


def _experts(xb, blk_e, n_used, w1, b1, w2, b2):
    n_slots = xb.shape[0]
    n_blocks = n_slots // EXPERT_BLOCK
    grid_spec = pltpu.PrefetchScalarGridSpec(
        num_scalar_prefetch=2, grid=(n_blocks,),
        in_specs=[
            pl.BlockSpec((EXPERT_BLOCK, D), lambda i, be, nb: (i, 0)),
            pl.BlockSpec((1, D, 2 * FF), lambda i, be, nb: (be[i], 0, 0)),
            pl.BlockSpec((1, 1, 2 * FF), lambda i, be, nb: (be[i], 0, 0)),
            pl.BlockSpec((1, FF, D), lambda i, be, nb: (be[i], 0, 0)),
            pl.BlockSpec((1, 1, D), lambda i, be, nb: (be[i], 0, 0)),
        ],
        out_specs=pl.BlockSpec((EXPERT_BLOCK, D), lambda i, be, nb: (i, 0)),
        scratch_shapes=[pltpu.VMEM((D, 2 * FF), BF16), pltpu.VMEM((FF, D), BF16)],
    )
    return pl.pallas_call(
        _expert_kernel, grid_spec=grid_spec,
        out_shape=jax.ShapeDtypeStruct((n_slots, D), BF16),
        compiler_params=_cparams(("arbitrary",)),
    )(blk_e, n_used, xb, w1, b1.reshape(N_EXP, 1, 2 * FF), w2, b2.reshape(N_EXP, 1, D))


def _combine_kernel(x_ref, yg_ref, cw_ref, mod_ref, o_ref):
    cw = cw_ref[...]
    f = jnp.zeros((TM, D), F32)
    for k in range(TOP_K):
        f = f + cw[:, k:k + 1] * yg_ref[:, k * D:(k + 1) * D].astype(F32)
    o_ref[...] = x_ref[...] + mod_ref[0, 5:6, :] * f


def _combine(x, yg, comb, mod_l, tiles_per_b, n_b):
    T = x.shape[0]
    return pl.pallas_call(
        _combine_kernel,
        out_shape=jax.ShapeDtypeStruct((T, D), F32),
        grid=(T // TM,),
        in_specs=[pl.BlockSpec((TM, D), lambda i: (i, 0)),
                  pl.BlockSpec((TM, TOP_K * D), lambda i: (i, 0)),
                  pl.BlockSpec((TM, TOP_K), lambda i: (i, 0)),
                  pl.BlockSpec((1, 6, D), lambda i: (_mod_row(i, tiles_per_b, n_b), 0, 0))],
        out_specs=pl.BlockSpec((TM, D), lambda i: (i, 0)),
        compiler_params=_cparams(("arbitrary",)),
    )(x, yg, comb, mod_l)


def _routing(logits):
    t = logits.shape[0]
    top_val, top_idx = lax.top_k(logits, TOP_K)
    comb = jax.nn.softmax(top_val, axis=-1)
    flat_e = top_idx.reshape(-1)
    n_assign = t * TOP_K
    order = jnp.argsort(flat_e)
    e_sorted = flat_e[order]
    tok_sorted = (order // TOP_K).astype(jnp.int32)
    counts = jnp.bincount(flat_e, length=N_EXP)
    padded = (counts + EXPERT_BLOCK - 1) // EXPERT_BLOCK * EXPERT_BLOCK
    pad_end = jnp.cumsum(padded)
    pad_start = pad_end - padded
    start = jnp.cumsum(counts) - counts
    slot_sorted = (pad_start[e_sorted] + jnp.arange(n_assign) - start[e_sorted]).astype(jnp.int32)
    n_blocks = -(-n_assign // EXPERT_BLOCK) + N_EXP
    n_slots = n_blocks * EXPERT_BLOCK
    slot_tok = jnp.full((n_slots,), t, jnp.int32).at[slot_sorted].set(tok_sorted)
    slot_of = jnp.zeros((n_assign,), jnp.int32).at[order].set(slot_sorted)
    blk_e = jnp.minimum(jnp.searchsorted(pad_end, jnp.arange(n_blocks) * EXPERT_BLOCK, side='right'),
                        N_EXP - 1).astype(jnp.int32)
    n_used = (pad_end[-1] // EXPERT_BLOCK).astype(jnp.int32).reshape(1)
    return comb, slot_tok, slot_of.reshape(t, TOP_K), blk_e, n_used


def _rope_tables(tiles_per_b):
    n_lat = (tiles_per_b - 1) * TM
    pos = jnp.arange(n_lat, dtype=F32)
    t_row = jnp.floor(pos / GRID_W)
    t_col = pos - t_row * GRID_W
    n_freq = HD // 4
    inv_freq = ROPE_THETA ** (-jnp.arange(n_freq, dtype=F32) / n_freq)
    lane = jnp.arange(LANES)
    d = lane % HD
    use_col = (d // 32) == 1
    f = d % n_freq
    first = (d % 32) < n_freq
    ang = jnp.where(use_col[None, :], t_col[:, None], t_row[:, None]) * inv_freq[f][None, :]
    cos = jnp.cos(ang)
    sin = jnp.where(first[None, :], -jnp.sin(ang), jnp.sin(ang))
    cos = jnp.concatenate([jnp.ones((TM, LANES), F32), cos], axis=0)
    sin = jnp.concatenate([jnp.zeros((TM, LANES), F32), sin], axis=0)
    return cos, sin


def _permute_w_in(w):
    gk, gv, dk, dv = w[:, 0:128], w[:, 128:256], w[:, 256:768], w[:, 768:1280]
    return jnp.concatenate([gk, dk, w[:, 1280:2304], gv, dv, w[:, 2304:]], axis=1).astype(BF16)


def kernel(x, c, ctx, c_ctx, w_mod, b_mod, norm1_g, norm2_g, w_in, gqa_q_norm, gqa_k_norm, diff_q_norm, diff_k_norm, lam_q1, lam_k1, lam_q2, lam_k2, diff_subln_g, conv_dw_w, conv_dw_b, conv_ln_g, conv_ln_b, conv_pw_w, w_gqa_o, w_diff_o, b_gate, w_out, router_w, router_b, exp_w1, exp_b1, exp_w2, exp_b2):
    n_b, n_lat, d = x.shape
    n_ctx = ctx.shape[1]
    L = w_mod.shape[0]
    assert d == D and n_ctx == TM and n_lat % TM == 0 and n_b < 8
    tiles_per_b = (n_ctx + n_lat) // TM
    T = n_b * tiles_per_b * TM

    xs = jnp.concatenate([ctx, x], axis=1).reshape(T, D)
    cc = jnp.zeros((8, D), F32).at[:n_b].set(c).at[n_b].set(c_ctx)
    mod = _modulation(cc, w_mod, b_mod).reshape(L, 8, 6, D)
    cos_t, sin_t = _rope_tables(tiles_per_b)
    q_scale = HD ** -0.5

    for l in range(L):
        lam_init = 0.8 - 0.6 * math.exp(-0.3 * l)
        mod_l = mod[l]
        tile2 = lambda g: jnp.tile(g, 2)
        qk_gain = jnp.concatenate(
            [tile2(gqa_k_norm[l])] * (GQA_KVH // 2) + [tile2(diff_k_norm[l])] * DIFF_H
            + [tile2(gqa_q_norm[l]) * q_scale] * (GQA_QH // 2)
            + [tile2(diff_q_norm[l]) * q_scale] * DIFF_H).reshape(1, QK_COLS)
        k3, qt, vt3, u, gates = _inproj(
            xs, mod_l, norm1_g[l].reshape(1, D), _permute_w_in(w_in[l]), cos_t, sin_t, qk_gain,
            b_gate[l].reshape(1, GATE_COLS), tiles_per_b, n_b)
        og = _attention(qt, k3, vt3, n_b, tiles_per_b, diff=False)
        od = _attention(qt, k3, vt3, n_b, tiles_per_b, diff=True,
                        lam_refs=[v[l].reshape(1, HD) for v in (lam_q1, lam_k1, lam_q2, lam_k2)],
                        subln_g=diff_subln_g[l].reshape(1, DIFF_V), lam_init=lam_init)
        dw_w = jnp.concatenate([conv_dw_w[l], jnp.zeros((1, CONV_CH), F32)], axis=0)
        xs, h2, logits = _mix(
            u, og, od, gates, xs, mod_l, dw_w, conv_dw_b[l].reshape(1, CONV_CH),
            conv_ln_g[l].reshape(1, CONV_CH), conv_ln_b[l].reshape(1, CONV_CH),
            conv_pw_w[l].astype(BF16), w_gqa_o[l].astype(BF16), w_diff_o[l].astype(BF16),
            w_out[l].astype(BF16), norm2_g[l].reshape(1, D), router_w[l],
            router_b[l].reshape(1, N_EXP), tiles_per_b, n_b)

        comb, slot_tok, slot_of, blk_e, n_used = _routing(logits)
        h2_pad = jnp.concatenate([h2, jnp.zeros((1, D), BF16)], axis=0)
        xb = h2_pad[slot_tok]
        yb = _experts(xb, blk_e, n_used, exp_w1[l], exp_b1[l], exp_w2[l], exp_b2[l])
        yg = yb[slot_of.reshape(-1)].reshape(T, TOP_K * D)
        xs = _combine(xs, yg, comb, mod_l, tiles_per_b, n_b)

    return xs.reshape(n_b, n_ctx + n_lat, D)[:, n_ctx:, :]
```

```python
import functools
import math

import jax
import jax.numpy as jnp
from jax import lax
from jax.experimental import pallas as pl
from jax.experimental.pallas import tpu as pltpu

F32 = jnp.float32
BF16 = jnp.bfloat16

D = 1024
HD = 64
GRID_W = 64
ROPE_THETA = 10000.0
EPS = 1e-6
CONV_CH = 512
CONV_W = 31
CONV_HALO = 16
GQA_QH = 8
GQA_KVH = 2
DIFF_H = 4
DIFF_V = 128
N_EXP = 32
TOP_K = 4
FF = 1024
SWIGLU_ALPHA = 1.702
SWIGLU_LIMIT = 7.0
EXPERT_BLOCK = 256
TM = 256
LANES = 128
SUM_ROWS = 16

N_KHEADS = GQA_KVH + 2 * DIFF_H
N_QHEADS = GQA_QH + 2 * DIFF_H
QK_COLS = (N_KHEADS + N_QHEADS) * HD
V_COLS = GQA_KVH * HD + DIFF_H * DIFF_V
GATE_COLS = 3 * D
D_IN = QK_COLS + V_COLS + 2 * CONV_CH + GATE_COLS
VMEM_LIMIT = 56 * 1024 * 1024


def _cparams(sem):
    return pltpu.CompilerParams(dimension_semantics=sem, vmem_limit_bytes=VMEM_LIMIT)


def _mod_kernel(c_ref, w_ref, b_ref, o_ref):
    c = c_ref[...]
    s = c * (1.0 / (1.0 + jnp.exp(-c)))
    o_ref[0] = jnp.dot(s, w_ref[0], preferred_element_type=F32,
                       precision=lax.Precision.HIGHEST) + b_ref[0]


def _modulation(cc, w_mod, b_mod):
    L = w_mod.shape[0]
    tn = 1536
    return pl.pallas_call(
        _mod_kernel,
        out_shape=jax.ShapeDtypeStruct((L, 8, 6 * D), F32),
        grid=(L, 6 * D // tn),
        in_specs=[pl.BlockSpec((8, D), lambda l, j: (0, 0)),
                  pl.BlockSpec((1, D, tn), lambda l, j: (l, 0, j)),
                  pl.BlockSpec((1, 1, tn), lambda l, j: (l, 0, j))],
        out_specs=pl.BlockSpec((1, 8, tn), lambda l, j: (l, 0, j)),
        compiler_params=_cparams(("arbitrary", "arbitrary")),
    )(cc, w_mod, b_mod.reshape(L, 1, 6 * D))


def _modnorm(x, g, shift, scale):
    ms = jnp.mean(x * x, axis=-1, keepdims=True)
    return (x * lax.rsqrt(ms + EPS) * g) * (1.0 + scale) + shift


def _inproj_kernel(x_ref, mod_ref, g1_ref, w_ref, cos_ref, sin_ref, qkg_ref, bg_ref,
                   kg_out, kd_out, qt_out, vtg_out, vtd_out, u_out, g_out):
    h = _modnorm(x_ref[...], g1_ref[...], mod_ref[0, 0:1, :], mod_ref[0, 1:2, :]).astype(BF16)
    cos = cos_ref[...]
    sin = sin_ref[...]
    lane = lax.broadcasted_iota(jnp.int32, (1, LANES), 1)
    first = (lane % 32) < 16
    r_i = lax.broadcasted_iota(jnp.int32, (LANES, LANES), 0) // HD
    c_i = lax.broadcasted_iota(jnp.int32, (LANES, LANES), 1) // HD
    head_ones = jnp.where(r_i == c_i, 1.0, 0.0).astype(BF16)

    def proj(c0, c1):
        return jnp.dot(h, w_ref[:, c0:c1], preferred_element_type=F32)

    n_pairs = QK_COLS // LANES
    pairs_per_dot = 4
    for p0 in range(0, n_pairs, pairs_per_dot):
        p1 = min(p0 + pairs_per_dot, n_pairs)
        z = proj(p0 * LANES, p1 * LANES)
        for p in range(p0, p1):
            zc = z[:, (p - p0) * LANES:(p - p0 + 1) * LANES]
            ss = jnp.dot((zc * zc).astype(BF16), head_ones, preferred_element_type=F32)
            n = zc * lax.rsqrt(ss * (1.0 / HD) + EPS) * qkg_ref[:, p * LANES:(p + 1) * LANES]
            partner = jnp.where(first, pltpu.roll(n, LANES - 16, 1), pltpu.roll(n, 16, 1))
            r = n * cos + partner * sin
            if 2 * p < GQA_KVH:
                kg_out[2 * p] = r[:, :HD].astype(BF16)
                kg_out[2 * p + 1] = r[:, HD:].astype(BF16)
            elif 2 * p < N_KHEADS:
                kd_out[2 * p - GQA_KVH] = r[:, :HD].astype(BF16)
                kd_out[2 * p - GQA_KVH + 1] = r[:, HD:].astype(BF16)
            else:
                rt = r.T
                qh = 2 * p - N_KHEADS
                qt_out[qh] = rt[:HD, :].astype(BF16)
                qt_out[qh + 1] = rt[HD:, :].astype(BF16)

    z = proj(QK_COLS, QK_COLS + V_COLS)
    gv_pairs = GQA_KVH * HD // LANES
    for p in range(V_COLS // LANES):
        zt = z[:, p * LANES:(p + 1) * LANES].T.astype(BF16)
        if p < gv_pairs:
            vtg_out[0, p * LANES:(p + 1) * LANES, :] = zt
        else:
            vtd_out[0, (p - gv_pairs) * LANES:(p - gv_pairs + 1) * LANES, :] = zt

    c0 = QK_COLS + V_COLS
    a = proj(c0, c0 + 2 * CONV_CH)
    gl = a[:, CONV_CH:]
    u_out[...] = a[:, :CONV_CH] * (1.0 / (1.0 + jnp.exp(-gl)))

    c0 = c0 + 2 * CONV_CH
    for j in range(3):
        gp = proj(c0 + j * D, c0 + (j + 1) * D) + bg_ref[:, j * D:(j + 1) * D]
        g_out[:, j * D:(j + 1) * D] = (1.0 / (1.0 + jnp.exp(-gp))).astype(BF16)


def _mod_row(i, tiles_per_b, n_b):
    return jnp.where(i % tiles_per_b == 0, n_b, i // tiles_per_b)


def _inproj(x, mod_l, g1, w_perm, cos_t, sin_t, qk_gain, b_gate, tiles_per_b, n_b):
    T = x.shape[0]
    nt = T // TM
    out_shape = (
        jax.ShapeDtypeStruct((GQA_KVH, T, HD), BF16),
        jax.ShapeDtypeStruct((2 * DIFF_H, T, HD), BF16),
        jax.ShapeDtypeStruct((N_QHEADS, HD, T), BF16),
        jax.ShapeDtypeStruct((nt, GQA_KVH * HD, TM), BF16),
        jax.ShapeDtypeStruct((nt, DIFF_H * DIFF_V, TM), BF16),
        jax.ShapeDtypeStruct((T, CONV_CH), F32),
        jax.ShapeDtypeStruct((T, GATE_COLS), BF16),
    )
    in_specs = [
        pl.BlockSpec((TM, D), lambda i: (i, 0)),
        pl.BlockSpec((1, 6, D), lambda i: (_mod_row(i, tiles_per_b, n_b), 0, 0)),
        pl.BlockSpec((1, D), lambda i: (0, 0)),
        pl.BlockSpec((D, D_IN), lambda i: (0, 0)),
        pl.BlockSpec((TM, LANES), lambda i: (i % tiles_per_b, 0)),
        pl.BlockSpec((TM, LANES), lambda i: (i % tiles_per_b, 0)),
        pl.BlockSpec((1, QK_COLS), lambda i: (0, 0)),
        pl.BlockSpec((1, GATE_COLS), lambda i: (0, 0)),
    ]
    out_specs = (
        pl.BlockSpec((GQA_KVH, TM, HD), lambda i: (0, i, 0)),
        pl.BlockSpec((2 * DIFF_H, TM, HD), lambda i: (0, i, 0)),
        pl.BlockSpec((N_QHEADS, HD, TM), lambda i: (0, 0, i)),
        pl.BlockSpec((1, GQA_KVH * HD, TM), lambda i: (i, 0, 0)),
        pl.BlockSpec((1, DIFF_H * DIFF_V, TM), lambda i: (i, 0, 0)),
        pl.BlockSpec((TM, CONV_CH), lambda i: (i, 0)),
        pl.BlockSpec((TM, GATE_COLS), lambda i: (i, 0)),
    )
    return pl.pallas_call(
        _inproj_kernel, out_shape=out_shape, grid=(nt,),
        in_specs=in_specs, out_specs=out_specs,
        compiler_params=_cparams(("arbitrary",)),
    )(x, mod_l, g1, w_perm, cos_t, sin_t, qk_gain, b_gate)


def _attn_kernel(*refs, n_maps, n_v, share_k, dv, diff, lam_init):
    maps_per_v = n_maps // n_v
    if diff:
        (qt_ref, k_ref, vt_ref, lq1, lk1, lq2, lk2, sg_ref, o_ref, s0_sc, s1_sc, m_sc, acc_sc) = refs
    else:
        (qt_ref, k_ref, vt_ref, o_ref, s0_sc, s1_sc, m_sc, acc_sc) = refs
    qi = pl.program_id(2)
    n_pairs = jnp.where(qi == 0, 0, (vt_ref.shape[0] - 1) // 2)

    m_sc[...] = jnp.full(m_sc.shape, -jnp.inf, F32)
    acc_sc[...] = jnp.zeros(acc_sc.shape, F32)
    ones = jnp.ones((SUM_ROWS, TM), BF16)

    def scores(j, s_sc):
        off = pl.multiple_of(j * TM, TM)
        for m in range(n_maps):
            kc = k_ref[0 if share_k else m, pl.ds(off, TM), :]
            s_sc[m] = jnp.dot(kc, qt_ref[m], preferred_element_type=F32)

    def softmax_pv(j, s_sc):
        vts = [jnp.concatenate([vt_ref[j, h * dv:(h + 1) * dv, :], ones], axis=0)
               for h in range(n_v)]
        for m in range(n_maps):
            vt = vts[m // maps_per_v]
            s = s_sc[m]
            m_prev = m_sc[m]
            m_new = jnp.maximum(m_prev, jnp.max(s, axis=0, keepdims=True))
            p = jnp.exp2(s - m_new).astype(BF16)
            alpha = jnp.exp2(m_prev - m_new)
            acc_sc[m] = alpha * acc_sc[m] + jnp.dot(vt, p, preferred_element_type=F32)
            m_sc[m] = m_new

    scores(0, s0_sc)

    def body(i, carry):
        j = 2 * i
        scores(j + 1, s1_sc)
        softmax_pv(j, s0_sc)
        scores(j + 2, s0_sc)
        softmax_pv(j + 1, s1_sc)
        return carry

    lax.fori_loop(0, n_pairs, body, 0)
    softmax_pv(2 * n_pairs, s0_sc)

    def out(m):
        return acc_sc[m, 0:dv, :] / acc_sc[m, dv:dv + 1, :]

    if diff:
        lam = (jnp.exp(jnp.sum(lq1[...] * lk1[...], axis=-1, keepdims=True))
               - jnp.exp(jnp.sum(lq2[...] * lk2[...], axis=-1, keepdims=True)) + lam_init)
        for h in range(n_v):
            o = (out(2 * h) - lam * out(2 * h + 1)).T
            ms = jnp.mean(o * o, axis=-1, keepdims=True)
            o_ref[:, h * dv:(h + 1) * dv] = (o * lax.rsqrt(ms + EPS) * sg_ref[...]
                                             * (1.0 - lam_init)).astype(o_ref.dtype)
    else:
        o = jnp.concatenate([out(m) for m in range(n_maps)], axis=0)
        o_ref[...] = o.T.astype(o_ref.dtype)


def _attention(qt, k3, vt3, n_b, tiles_per_b, *, diff, lam_refs=None, subln_g=None, lam_init=0.0):
    T = qt.shape[2]
    n_maps = 4
    if diff:
        n_groups, n_v, share_k, dv = DIFF_H // 2, 2, False, DIFF_V
        q_unit0 = GQA_QH // n_maps
        k_blk = n_maps
    else:
        n_groups, n_v, share_k, dv = GQA_KVH, 1, True, HD
        q_unit0 = 0
        k_blk = 1
    vt4 = vt3.reshape(n_b, tiles_per_b, vt3.shape[1], TM)
    tokens_per_b = tiles_per_b * TM
    in_specs = [
        pl.BlockSpec((n_maps, HD, TM), lambda b, g, q: (q_unit0 + g, 0, b * tiles_per_b + q)),
        pl.BlockSpec((k_blk, tokens_per_b, HD), lambda b, g, q: (g, b, 0)),
        pl.BlockSpec((None, tiles_per_b, n_v * dv, TM), lambda b, g, q: (b, 0, g, 0)),
    ]
    args = [qt, k3, vt4]
    if diff:
        in_specs += [pl.BlockSpec((1, HD), lambda b, g, q: (0, 0))] * 4
        in_specs += [pl.BlockSpec((1, DIFF_V), lambda b, g, q: (0, 0))]
        args += list(lam_refs) + [subln_g]
    out_cols = n_v * dv if diff else n_maps * dv
    kern = functools.partial(_attn_kernel, n_maps=n_maps, n_v=n_v, share_k=share_k, dv=dv, diff=diff,
                             lam_init=lam_init)
    return pl.pallas_call(
        kern,
        out_shape=jax.ShapeDtypeStruct((T, n_groups * out_cols), BF16),
        grid=(n_b, n_groups, tiles_per_b),
        in_specs=in_specs,
        out_specs=pl.BlockSpec((TM, out_cols), lambda b, g, q: (b * tiles_per_b + q, g)),
        scratch_shapes=[pltpu.VMEM((n_maps, TM, TM), F32), pltpu.VMEM((n_maps, TM, TM), F32),
                        pltpu.VMEM((n_maps, 1, TM), F32),
                        pltpu.VMEM((n_maps, dv + SUM_ROWS, TM), F32)],
        compiler_params=_cparams(("arbitrary", "arbitrary", "arbitrary")),
    )(*args)


def _mix_kernel(up_ref, uc_ref, un_ref, og_ref, od_ref, gt_ref, x_ref, mod_ref,
                dww_ref, dwb_ref, lng_ref, lnb_ref, pw_ref, wgo_ref, wdo_ref, wout_ref,
                g2_ref, wr_ref, br_ref,
                xo_ref, h2_ref, lg_ref, ext_sc, *, tiles_per_b):
    i = pl.program_id(0)
    ti = i % tiles_per_b
    has_prev = ti > 1
    has_next = jnp.logical_and(ti > 0, ti < tiles_per_b - 1)
    ext_sc[0:CONV_HALO, :] = jnp.where(has_prev, up_ref[...], 0.0)
    ext_sc[CONV_HALO:CONV_HALO + TM, :] = uc_ref[...]
    ext_sc[CONV_HALO + TM:, :] = jnp.where(has_next, un_ref[...], 0.0)
    acc = jnp.zeros((TM, CONV_CH), F32) + dwb_ref[...]
    base = CONV_HALO - CONV_W // 2
    for w in range(CONV_W):
        acc = acc + ext_sc[base + w:base + w + TM, :] * dww_ref[w:w + 1, :]
    mu = jnp.mean(acc, axis=-1, keepdims=True)
    xc = acc - mu
    var = jnp.mean(xc * xc, axis=-1, keepdims=True)
    ln = xc * lax.rsqrt(var + EPS) * lng_ref[...] + lnb_ref[...]
    act = ln * (1.0 / (1.0 + jnp.exp(-ln)))
    br_a = jnp.dot(act.astype(BF16), pw_ref[...], preferred_element_type=F32)
    br_b = jnp.dot(og_ref[...], wgo_ref[...], preferred_element_type=F32)
    br_c = jnp.dot(od_ref[...], wdo_ref[...], preferred_element_type=F32)
    merged = (gt_ref[:, 0:D].astype(F32) * br_a + gt_ref[:, D:2 * D].astype(F32) * br_b
              + gt_ref[:, 2 * D:3 * D].astype(F32) * br_c)
    y = jnp.dot(merged.astype(BF16), wout_ref[...], preferred_element_type=F32)
    x_new = x_ref[...] + mod_ref[0, 2:3, :] * y
    xo_ref[...] = x_new
    h2 = _modnorm(x_new, g2_ref[...], mod_ref[0, 3:4, :], mod_ref[0, 4:5, :])
    h2_ref[...] = h2.astype(BF16)
    lg_ref[...] = jnp.dot(h2, wr_ref[...], preferred_element_type=F32,
                          precision=lax.Precision.HIGHEST) + br_ref[...]


def _mix(u, og, od, gates, x, mod_l, dw_w, dw_b, ln_g, ln_b, pw_w, w_gqa_o, w_diff_o, w_out,
         g2, w_r, b_r, tiles_per_b, n_b):
    T = x.shape[0]
    nt = T // TM
    halo_per_tile = TM // CONV_HALO
    n_halo = T // CONV_HALO
    full = lambda shape: pl.BlockSpec(shape, lambda i: (0,) * len(shape))
    in_specs = [
        pl.BlockSpec((CONV_HALO, CONV_CH), lambda i: (jnp.maximum(i * halo_per_tile - 1, 0), 0)),
        pl.BlockSpec((TM, CONV_CH), lambda i: (i, 0)),
        pl.BlockSpec((CONV_HALO, CONV_CH),
                     lambda i: (jnp.minimum((i + 1) * halo_per_tile, n_halo - 1), 0)),
        pl.BlockSpec((TM, GQA_QH * HD), lambda i: (i, 0)),
        pl.BlockSpec((TM, DIFF_H * DIFF_V), lambda i: (i, 0)),
        pl.BlockSpec((TM, GATE_COLS), lambda i: (i, 0)),
        pl.BlockSpec((TM, D), lambda i: (i, 0)),
        pl.BlockSpec((1, 6, D), lambda i: (_mod_row(i, tiles_per_b, n_b), 0, 0)),
        full((CONV_W + 1, CONV_CH)), full((1, CONV_CH)), full((1, CONV_CH)), full((1, CONV_CH)),
        full((CONV_CH, D)), full((GQA_QH * HD, D)), full((DIFF_H * DIFF_V, D)), full((D, D)),
        full((1, D)), full((D, N_EXP)), full((1, N_EXP)),
    ]
    out_shape = (jax.ShapeDtypeStruct((T, D), F32), jax.ShapeDtypeStruct((T, D), BF16),
                 jax.ShapeDtypeStruct((T, N_EXP), F32))
    out_specs = (pl.BlockSpec((TM, D), lambda i: (i, 0)), pl.BlockSpec((TM, D), lambda i: (i, 0)),
                 pl.BlockSpec((TM, N_EXP), lambda i: (i, 0)))
    return pl.pallas_call(
        functools.partial(_mix_kernel, tiles_per_b=tiles_per_b),
        out_shape=out_shape, grid=(nt,), in_specs=in_specs, out_specs=out_specs,
        scratch_shapes=[pltpu.VMEM((TM + 2 * CONV_HALO, CONV_CH), F32)],
        compiler_params=_cparams(("arbitrary",)),
    )(u, u, u, og, od, gates, x, mod_l, dw_w, dw_b, ln_g, ln_b, pw_w, w_gqa_o, w_diff_o, w_out,
      g2, w_r, b_r)


def _expert_kernel(be_ref, nb_ref, x_ref, w1_ref, b1_ref, w2_ref, b2_ref, y_ref, w1_sc, w2_sc):
    i = pl.program_id(0)
    prev_e = be_ref[jnp.maximum(i - 1, 0)]
    new_expert = jnp.logical_or(i == 0, be_ref[i] != prev_e)

    @pl.when(new_expert)
    def _():
        w1_sc[...] = w1_ref[0].astype(BF16)
        w2_sc[...] = w2_ref[0].astype(BF16)

    @pl.when(i < nb_ref[0])
    def _():
        gu = jnp.dot(x_ref[...], w1_sc[...], preferred_element_type=F32) + b1_ref[0]
        gate = jnp.minimum(gu[:, :FF], SWIGLU_LIMIT)
        lin = jnp.clip(gu[:, FF:], -SWIGLU_LIMIT, SWIGLU_LIMIT)
        act = (lin + 1.0) * (gate * (1.0 / (1.0 + jnp.exp(-gate * SWIGLU_ALPHA))))
        y = jnp.dot(act.astype(BF16), w2_sc[...], preferred_element_type=F32) + b2_ref[0]
        y_ref[...] = y.astype(y_ref.dtype)

    @pl.when(i >= nb_ref[0])
    def _():
        y_ref[...] = jnp.zeros(y_ref.shape, y_ref.dtype)


def _experts(xb, blk_e, n_used, w1, b1, w2, b2, l):
    n_slots = xb.shape[0]
    n_blocks = n_slots // EXPERT_BLOCK
    L = w1.shape[0]
    grid_spec = pltpu.PrefetchScalarGridSpec(
        num_scalar_prefetch=2, grid=(n_blocks,),
        in_specs=[
            pl.BlockSpec((EXPERT_BLOCK, D), lambda i, be, nb: (i, 0)),
            pl.BlockSpec((None, 1, D, 2 * FF), lambda i, be, nb: (l, be[i], 0, 0)),
            pl.BlockSpec((None, 1, 1, 2 * FF), lambda i, be, nb: (l, be[i], 0, 0)),
            pl.BlockSpec((None, 1, FF, D), lambda i, be, nb: (l, be[i], 0, 0)),
            pl.BlockSpec((None, 1, 1, D), lambda i, be, nb: (l, be[i], 0, 0)),
        ],
        out_specs=pl.BlockSpec((EXPERT_BLOCK, D), lambda i, be, nb: (i, 0)),
        scratch_shapes=[pltpu.VMEM((D, 2 * FF), BF16), pltpu.VMEM((FF, D), BF16)],
    )
    return pl.pallas_call(
        _expert_kernel, grid_spec=grid_spec,
        out_shape=jax.ShapeDtypeStruct((n_slots, D), BF16),
        compiler_params=_cparams(("arbitrary",)),
    )(blk_e, n_used, xb, w1, b1.reshape(L, N_EXP, 1, 2 * FF), w2, b2.reshape(L, N_EXP, 1, D))


def _combine_kernel(x_ref, yg_ref, cw_ref, mod_ref, o_ref):
    cw = cw_ref[...]
    f = jnp.zeros((TM, D), F32)
    for k in range(TOP_K):
        f = f + cw[:, k:k + 1] * yg_ref[k].astype(F32)
    o_ref[...] = x_ref[...] + mod_ref[0, 5:6, :] * f


def _combine(x, yg, comb, mod_l, tiles_per_b, n_b):
    T = x.shape[0]
    return pl.pallas_call(
        _combine_kernel,
        out_shape=jax.ShapeDtypeStruct((T, D), F32),
        grid=(T // TM,),
        in_specs=[pl.BlockSpec((TM, D), lambda i: (i, 0)),
                  pl.BlockSpec((TOP_K, TM, D), lambda i: (0, i, 0)),
                  pl.BlockSpec((TM, TOP_K), lambda i: (i, 0)),
                  pl.BlockSpec((1, 6, D), lambda i: (_mod_row(i, tiles_per_b, n_b), 0, 0))],
        out_specs=pl.BlockSpec((TM, D), lambda i: (i, 0)),
        compiler_params=_cparams(("arbitrary",)),
    )(x, yg, comb, mod_l)


def _routing(logits):
    t = logits.shape[0]
    top_val, top_idx = lax.top_k(logits, TOP_K)
    comb = jax.nn.softmax(top_val, axis=-1)
    flat_e = top_idx.reshape(-1)
    n_assign = t * TOP_K
    order = jnp.argsort(flat_e)
    e_sorted = flat_e[order]
    tok_sorted = (order // TOP_K).astype(jnp.int32)
    counts = jnp.sum((flat_e[:, None] == jnp.arange(N_EXP)[None, :]).astype(jnp.int32), axis=0)
    padded = (counts + EXPERT_BLOCK - 1) // EXPERT_BLOCK * EXPERT_BLOCK
    pad_end = jnp.cumsum(padded)
    pad_start = pad_end - padded
    start = jnp.cumsum(counts) - counts
    slot_sorted = (pad_start[e_sorted] + jnp.arange(n_assign) - start[e_sorted]).astype(jnp.int32)
    n_blocks = -(-n_assign // EXPERT_BLOCK) + N_EXP
    n_slots = n_blocks * EXPERT_BLOCK
    slot_tok = jnp.full((n_slots,), t, jnp.int32).at[slot_sorted].set(tok_sorted)
    slot_of = jnp.zeros((n_assign,), jnp.int32).at[order].set(slot_sorted)
    blk_start = jnp.arange(n_blocks, dtype=jnp.int32) * EXPERT_BLOCK
    blk_e = jnp.minimum(jnp.sum((pad_end[None, :] <= blk_start[:, None]).astype(jnp.int32), axis=1),
                        N_EXP - 1).astype(jnp.int32)
    n_used = (pad_end[-1] // EXPERT_BLOCK).astype(jnp.int32).reshape(1)
    return comb, slot_tok, slot_of.reshape(t, TOP_K).T.reshape(-1), blk_e, n_used


def _rope_tables(tiles_per_b):
    n_lat = (tiles_per_b - 1) * TM
    pos = jnp.arange(n_lat, dtype=F32)
    t_row = jnp.floor(pos / GRID_W)
    t_col = pos - t_row * GRID_W
    n_freq = HD // 4
    inv_freq = ROPE_THETA ** (-jnp.arange(n_freq, dtype=F32) / n_freq)
    lane = jnp.arange(LANES)
    d = lane % HD
    use_col = (d // 32) == 1
    f = d % n_freq
    first = (d % 32) < n_freq
    ang = jnp.where(use_col[None, :], t_col[:, None], t_row[:, None]) * inv_freq[f][None, :]
    cos = jnp.cos(ang)
    sin = jnp.where(first[None, :], -jnp.sin(ang), jnp.sin(ang))
    cos = jnp.concatenate([jnp.ones((TM, LANES), F32), cos], axis=0)
    sin = jnp.concatenate([jnp.zeros((TM, LANES), F32), sin], axis=0)
    return cos, sin


def _permute_w_in(w):
    gk, gv, dk, dv = w[:, 0:128], w[:, 128:256], w[:, 256:768], w[:, 768:1280]
    return jnp.concatenate([gk, dk, w[:, 1280:2304], gv, dv, w[:, 2304:]], axis=1).astype(BF16)


def kernel(x, c, ctx, c_ctx, w_mod, b_mod, norm1_g, norm2_g, w_in, gqa_q_norm, gqa_k_norm, diff_q_norm, diff_k_norm, lam_q1, lam_k1, lam_q2, lam_k2, diff_subln_g, conv_dw_w, conv_dw_b, conv_ln_g, conv_ln_b, conv_pw_w, w_gqa_o, w_diff_o, b_gate, w_out, router_w, router_b, exp_w1, exp_b1, exp_w2, exp_b2):
    n_b, n_lat, d = x.shape
    n_ctx = ctx.shape[1]
    L = w_mod.shape[0]
    assert d == D and n_ctx == TM and n_lat % TM == 0 and n_b < 8
    tiles_per_b = (n_ctx + n_lat) // TM
    T = n_b * tiles_per_b * TM

    xs = jnp.concatenate([ctx, x], axis=1).reshape(T, D)
    cc = jnp.zeros((8, D), F32).at[:n_b].set(c).at[n_b].set(c_ctx)
    mod = _modulation(cc, w_mod, b_mod).reshape(L, 8, 6, D)
    cos_t, sin_t = _rope_tables(tiles_per_b)
    q_scale = HD ** -0.5 * math.log2(math.e)

    for l in range(L):
        lam_init = 0.8 - 0.6 * math.exp(-0.3 * l)
        mod_l = mod[l]
        tile2 = lambda g: jnp.tile(g, 2)
        qk_gain = jnp.concatenate(
            [tile2(gqa_k_norm[l])] * (GQA_KVH // 2) + [tile2(diff_k_norm[l])] * DIFF_H
            + [tile2(gqa_q_norm[l]) * q_scale] * (GQA_QH // 2)
            + [tile2(diff_q_norm[l]) * q_scale] * DIFF_H).reshape(1, QK_COLS)
        kg, kd, qt, vtg, vtd, u, gates = _inproj(
            xs, mod_l, norm1_g[l].reshape(1, D), _permute_w_in(w_in[l]), cos_t, sin_t, qk_gain,
            b_gate[l].reshape(1, GATE_COLS), tiles_per_b, n_b)
        og = _attention(qt, kg, vtg, n_b, tiles_per_b, diff=False)
        od = _attention(qt, kd, vtd, n_b, tiles_per_b, diff=True,
                        lam_refs=[v[l].reshape(1, HD) for v in (lam_q1, lam_k1, lam_q2, lam_k2)],
                        subln_g=diff_subln_g[l].reshape(1, DIFF_V), lam_init=lam_init)
        dw_w = jnp.concatenate([conv_dw_w[l], jnp.zeros((1, CONV_CH), F32)], axis=0)
        xs, h2, logits = _mix(
            u, og, od, gates, xs, mod_l, dw_w, conv_dw_b[l].reshape(1, CONV_CH),
            conv_ln_g[l].reshape(1, CONV_CH), conv_ln_b[l].reshape(1, CONV_CH),
            conv_pw_w[l].astype(BF16), w_gqa_o[l].astype(BF16), w_diff_o[l].astype(BF16),
            w_out[l].astype(BF16), norm2_g[l].reshape(1, D), router_w[l],
            router_b[l].reshape(1, N_EXP), tiles_per_b, n_b)

        comb, slot_tok, slot_of, blk_e, n_used = _routing(logits)
        h2_pad = jnp.concatenate([h2, jnp.zeros((1, D), BF16)], axis=0)
        xb = h2_pad[slot_tok]
        yb = _experts(xb, blk_e, n_used, exp_w1, exp_b1, exp_w2, exp_b2, l)
        yg = yb[slot_of].reshape(TOP_K, T, D)
        xs = _combine(xs, yg, comb, mod_l, tiles_per_b, n_b)

    return xs.reshape(n_b, n_ctx + n_lat, D)[:, n_ctx:, :]
```

```python
import functools
import math

import jax
import jax.numpy as jnp
from jax import lax
from jax.experimental import pallas as pl
from jax.experimental.pallas import tpu as pltpu

F32 = jnp.float32
BF16 = jnp.bfloat16

D = 1024
HD = 64
GRID_W = 64
ROPE_THETA = 10000.0
EPS = 1e-6
CONV_CH = 512
CONV_W = 31
CONV_HALO = 16
GQA_QH = 8
GQA_KVH = 2
DIFF_H = 4
DIFF_V = 128
N_EXP = 32
TOP_K = 4
FF = 1024
SWIGLU_ALPHA = 1.702
SWIGLU_LIMIT = 7.0
EXPERT_BLOCK = 256
FF_CHUNK = 1024
TM = 256
LANES = 128
SUBLANES = 8
SUM_ROWS = 16
UNROLL = 8

N_KHEADS = GQA_KVH + 2 * DIFF_H
N_QHEADS = GQA_QH + 2 * DIFF_H
QK_COLS = (N_KHEADS + N_QHEADS) * HD
V_COLS = GQA_KVH * HD + DIFF_H * DIFF_V
GATE_COLS = 3 * D
D_IN = QK_COLS + V_COLS + 2 * CONV_CH + GATE_COLS
VMEM_LIMIT = 56 * 1024 * 1024


def _cparams(sem):
    return pltpu.CompilerParams(dimension_semantics=sem, vmem_limit_bytes=VMEM_LIMIT)


def _mod_kernel(c_ref, w_ref, b_ref, o_ref):
    c = c_ref[...]
    s = c * (1.0 / (1.0 + jnp.exp(-c)))
    o_ref[0] = jnp.dot(s, w_ref[0], preferred_element_type=F32,
                       precision=lax.Precision.HIGHEST) + b_ref[0]


def _modulation(cc, w_mod, b_mod):
    L = w_mod.shape[0]
    tn = 1536
    return pl.pallas_call(
        _mod_kernel,
        out_shape=jax.ShapeDtypeStruct((L, 8, 6 * D), F32),
        grid=(L, 6 * D // tn),
        in_specs=[pl.BlockSpec((8, D), lambda l, j: (0, 0)),
                  pl.BlockSpec((1, D, tn), lambda l, j: (l, 0, j)),
                  pl.BlockSpec((1, 1, tn), lambda l, j: (l, 0, j))],
        out_specs=pl.BlockSpec((1, 8, tn), lambda l, j: (l, 0, j)),
        compiler_params=_cparams(("arbitrary", "arbitrary")),
    )(cc, w_mod, b_mod.reshape(L, 1, 6 * D))


def _modnorm(x, g, shift, scale):
    ms = jnp.mean(x * x, axis=-1, keepdims=True)
    return (x * lax.rsqrt(ms + EPS) * g) * (1.0 + scale) + shift


def _inproj_kernel(x_ref, mod_ref, g1_ref, w_ref, cos_ref, sin_ref, qkg_ref, bg_ref,
                   kg_out, kd_out, qt_out, vtg_out, vtd_out, u_out, g_out):
    h = _modnorm(x_ref[...], g1_ref[...], mod_ref[0, 0:1, :], mod_ref[0, 1:2, :]).astype(BF16)
    cos = cos_ref[...]
    sin = sin_ref[...]
    lane = lax.broadcasted_iota(jnp.int32, (1, LANES), 1)
    first = (lane % 32) < 16
    r_i = lax.broadcasted_iota(jnp.int32, (LANES, LANES), 0) // HD
    c_i = lax.broadcasted_iota(jnp.int32, (LANES, LANES), 1) // HD
    head_ones = jnp.where(r_i == c_i, 1.0, 0.0).astype(BF16)

    def proj(c0, c1):
        return jnp.dot(h, w_ref[:, c0:c1], preferred_element_type=F32)

    n_pairs = QK_COLS // LANES
    pairs_per_dot = 4
    for p0 in range(0, n_pairs, pairs_per_dot):
        p1 = min(p0 + pairs_per_dot, n_pairs)
        z = proj(p0 * LANES, p1 * LANES)
        for p in range(p0, p1):
            zc = z[:, (p - p0) * LANES:(p - p0 + 1) * LANES]
            ss = jnp.dot((zc * zc).astype(BF16), head_ones, preferred_element_type=F32)
            n = zc * lax.rsqrt(ss * (1.0 / HD) + EPS) * qkg_ref[:, p * LANES:(p + 1) * LANES]
            partner = jnp.where(first, pltpu.roll(n, LANES - 16, 1), pltpu.roll(n, 16, 1))
            r = n * cos + partner * sin
            if 2 * p < GQA_KVH:
                kg_out[2 * p] = r[:, :HD].astype(BF16)
                kg_out[2 * p + 1] = r[:, HD:].astype(BF16)
            elif 2 * p < N_KHEADS:
                kd_out[2 * p - GQA_KVH] = r[:, :HD].astype(BF16)
                kd_out[2 * p - GQA_KVH + 1] = r[:, HD:].astype(BF16)
            else:
                rt = r.T
                qh = 2 * p - N_KHEADS
                qt_out[qh] = rt[:HD, :].astype(BF16)
                qt_out[qh + 1] = rt[HD:, :].astype(BF16)

    z = proj(QK_COLS, QK_COLS + V_COLS)
    gv_pairs = GQA_KVH * HD // LANES
    for p in range(V_COLS // LANES):
        zt = z[:, p * LANES:(p + 1) * LANES].T.astype(BF16)
        if p < gv_pairs:
            vtg_out[0, p * LANES:(p + 1) * LANES, :] = zt
        else:
            vtd_out[0, (p - gv_pairs) * LANES:(p - gv_pairs + 1) * LANES, :] = zt

    c0 = QK_COLS + V_COLS
    a = proj(c0, c0 + 2 * CONV_CH)
    gl = a[:, CONV_CH:]
    u_out[...] = a[:, :CONV_CH] * (1.0 / (1.0 + jnp.exp(-gl)))

    c0 = c0 + 2 * CONV_CH
    for j in range(3):
        gp = proj(c0 + j * D, c0 + (j + 1) * D) + bg_ref[:, j * D:(j + 1) * D]
        g_out[:, j * D:(j + 1) * D] = (1.0 / (1.0 + jnp.exp(-gp))).astype(BF16)


def _mod_row(i, tiles_per_b, n_b):
    return jnp.where(i % tiles_per_b == 0, n_b, i // tiles_per_b)


def _inproj(x, mod_l, g1, w_perm, cos_t, sin_t, qk_gain, b_gate, tiles_per_b, n_b):
    T = x.shape[0]
    nt = T // TM
    out_shape = (
        jax.ShapeDtypeStruct((GQA_KVH, T, HD), BF16),
        jax.ShapeDtypeStruct((2 * DIFF_H, T, HD), BF16),
        jax.ShapeDtypeStruct((N_QHEADS, HD, T), BF16),
        jax.ShapeDtypeStruct((nt, GQA_KVH * HD, TM), BF16),
        jax.ShapeDtypeStruct((nt, DIFF_H * DIFF_V, TM), BF16),
        jax.ShapeDtypeStruct((T, CONV_CH), F32),
        jax.ShapeDtypeStruct((T, GATE_COLS), BF16),
    )
    in_specs = [
        pl.BlockSpec((TM, D), lambda i: (i, 0)),
        pl.BlockSpec((1, 6, D), lambda i: (_mod_row(i, tiles_per_b, n_b), 0, 0)),
        pl.BlockSpec((1, D), lambda i: (0, 0)),
        pl.BlockSpec((D, D_IN), lambda i: (0, 0)),
        pl.BlockSpec((TM, LANES), lambda i: (i % tiles_per_b, 0)),
        pl.BlockSpec((TM, LANES), lambda i: (i % tiles_per_b, 0)),
        pl.BlockSpec((1, QK_COLS), lambda i: (0, 0)),
        pl.BlockSpec((1, GATE_COLS), lambda i: (0, 0)),
    ]
    out_specs = (
        pl.BlockSpec((GQA_KVH, TM, HD), lambda i: (0, i, 0)),
        pl.BlockSpec((2 * DIFF_H, TM, HD), lambda i: (0, i, 0)),
        pl.BlockSpec((N_QHEADS, HD, TM), lambda i: (0, 0, i)),
        pl.BlockSpec((1, GQA_KVH * HD, TM), lambda i: (i, 0, 0)),
        pl.BlockSpec((1, DIFF_H * DIFF_V, TM), lambda i: (i, 0, 0)),
        pl.BlockSpec((TM, CONV_CH), lambda i: (i, 0)),
        pl.BlockSpec((TM, GATE_COLS), lambda i: (i, 0)),
    )
    return pl.pallas_call(
        _inproj_kernel, out_shape=out_shape, grid=(nt,),
        in_specs=in_specs, out_specs=out_specs,
        compiler_params=_cparams(("arbitrary",)),
    )(x, mod_l, g1, w_perm, cos_t, sin_t, qk_gain, b_gate)


def _attn_kernel(*refs, n_maps, n_v, share_k, dv, diff, lam_init):
    maps_per_v = n_maps // n_v
    if diff:
        (qt_ref, k_ref, vt_ref, lq1, lk1, lq2, lk2, sg_ref, o_ref, s0_sc, s1_sc, m_sc, acc_sc) = refs
    else:
        (qt_ref, k_ref, vt_ref, o_ref, s0_sc, s1_sc, m_sc, acc_sc) = refs
    qi = pl.program_id(2)
    n_iter = jnp.where(qi == 0, 0, (vt_ref.shape[0] - 1) // UNROLL)

    m_sc[...] = jnp.full(m_sc.shape, -jnp.inf, F32)
    acc_sc[...] = jnp.zeros(acc_sc.shape, F32)
    ones = jnp.ones((SUM_ROWS, TM), BF16)

    def scores(j, s_sc):
        off = pl.multiple_of(j * TM, TM)
        for m in range(n_maps):
            kc = k_ref[0 if share_k else m, pl.ds(off, TM), :]
            s_sc[m] = jnp.dot(kc, qt_ref[m], preferred_element_type=F32)

    def softmax_pv(j, s_sc):
        vts = [jnp.concatenate([vt_ref[j, h * dv:(h + 1) * dv, :], ones], axis=0)
               for h in range(n_v)]
        for m in range(n_maps):
            vt = vts[m // maps_per_v]
            s = s_sc[m]
            m_prev = m_sc[m]
            m_new = jnp.maximum(m_prev, jnp.max(s, axis=0, keepdims=True))
            p = jnp.exp2(s - m_new).astype(BF16)
            alpha = jnp.exp2(m_prev - m_new)
            acc_sc[m] = alpha * acc_sc[m] + jnp.dot(vt, p, preferred_element_type=F32)
            m_sc[m] = m_new

    scores(0, s0_sc)

    def body(i, carry):
        j = UNROLL * i
        for u in range(0, UNROLL, 2):
            scores(j + u + 1, s1_sc)
            softmax_pv(j + u, s0_sc)
            scores(j + u + 2, s0_sc)
            softmax_pv(j + u + 1, s1_sc)
        return carry

    lax.fori_loop(0, n_iter, body, 0)
    softmax_pv(UNROLL * n_iter, s0_sc)

    def out(m):
        return acc_sc[m, 0:dv, :] / acc_sc[m, dv:dv + 1, :]

    if diff:
        lam = (jnp.exp(jnp.sum(lq1[...] * lk1[...], axis=-1, keepdims=True))
               - jnp.exp(jnp.sum(lq2[...] * lk2[...], axis=-1, keepdims=True)) + lam_init)
        for h in range(n_v):
            o = (out(2 * h) - lam * out(2 * h + 1)).T
            ms = jnp.mean(o * o, axis=-1, keepdims=True)
            o_ref[:, h * dv:(h + 1) * dv] = (o * lax.rsqrt(ms + EPS) * sg_ref[...]
                                             * (1.0 - lam_init)).astype(o_ref.dtype)
    else:
        o = jnp.concatenate([out(m) for m in range(n_maps)], axis=0)
        o_ref[...] = o.T.astype(o_ref.dtype)


def _attention(qt, k3, vt3, n_b, tiles_per_b, *, diff, lam_refs=None, subln_g=None, lam_init=0.0):
    T = qt.shape[2]
    n_maps = 4
    if diff:
        n_groups, n_v, share_k, dv = DIFF_H // 2, 2, False, DIFF_V
        q_unit0 = GQA_QH // n_maps
        k_blk = n_maps
    else:
        n_groups, n_v, share_k, dv = GQA_KVH, 1, True, HD
        q_unit0 = 0
        k_blk = 1
    vt4 = vt3.reshape(n_b, tiles_per_b, vt3.shape[1], TM)
    tokens_per_b = tiles_per_b * TM
    in_specs = [
        pl.BlockSpec((n_maps, HD, TM), lambda b, g, q: (q_unit0 + g, 0, b * tiles_per_b + q)),
        pl.BlockSpec((k_blk, tokens_per_b, HD), lambda b, g, q: (g, b, 0)),
        pl.BlockSpec((None, tiles_per_b, n_v * dv, TM), lambda b, g, q: (b, 0, g, 0)),
    ]
    args = [qt, k3, vt4]
    if diff:
        in_specs += [pl.BlockSpec((1, HD), lambda b, g, q: (0, 0))] * 4
        in_specs += [pl.BlockSpec((1, DIFF_V), lambda b, g, q: (0, 0))]
        args += list(lam_refs) + [subln_g]
    out_cols = n_v * dv if diff else n_maps * dv
    kern = functools.partial(_attn_kernel, n_maps=n_maps, n_v=n_v, share_k=share_k, dv=dv, diff=diff,
                             lam_init=lam_init)
    return pl.pallas_call(
        kern,
        out_shape=jax.ShapeDtypeStruct((T, n_groups * out_cols), BF16),
        grid=(n_b, n_groups, tiles_per_b),
        in_specs=in_specs,
        out_specs=pl.BlockSpec((TM, out_cols), lambda b, g, q: (b * tiles_per_b + q, g)),
        scratch_shapes=[pltpu.VMEM((n_maps, TM, TM), F32), pltpu.VMEM((n_maps, TM, TM), F32),
                        pltpu.VMEM((n_maps, 1, TM), F32),
                        pltpu.VMEM((n_maps, dv + SUM_ROWS, TM), F32)],
        compiler_params=_cparams(("arbitrary", "arbitrary", "arbitrary")),
    )(*args)


def _mix_kernel(up_ref, uc_ref, un_ref, og_ref, od_ref, gt_ref, x_ref, mod_ref,
                dww_ref, dwb_ref, lng_ref, lnb_ref, pw_ref, wgo_ref, wdo_ref, wout_ref,
                g2_ref, wr_ref, br_ref,
                xo_ref, h2_ref, lg_ref, ext_sc, sh_sc, *, tiles_per_b):
    i = pl.program_id(0)
    ti = i % tiles_per_b
    has_prev = ti > 1
    has_next = jnp.logical_and(ti > 0, ti < tiles_per_b - 1)
    ext_sc[0:CONV_HALO, :] = jnp.where(has_prev, up_ref[...], 0.0)
    ext_sc[CONV_HALO:CONV_HALO + TM, :] = uc_ref[...]
    ext_sc[CONV_HALO + TM:, :] = jnp.where(has_next, un_ref[...], 0.0)
    acc = jnp.zeros((TM, CONV_CH), F32) + dwb_ref[...]
    base = CONV_HALO - CONV_W // 2
    span = TM + (CONV_W // SUBLANES) * SUBLANES
    for r in range(SUBLANES):
        taps = [w for w in range(CONV_W) if (base + w) % SUBLANES == r]
        if not taps:
            continue
        if r == 0:
            src = ext_sc
        else:
            sh_sc[r - 1] = ext_sc[r:r + span, :]
            src = sh_sc.at[r - 1]
        for w in taps:
            a = (base + w) // SUBLANES * SUBLANES
            acc = acc + src[a:a + TM, :] * dww_ref[w:w + 1, :]
    mu = jnp.mean(acc, axis=-1, keepdims=True)
    xc = acc - mu
    var = jnp.mean(xc * xc, axis=-1, keepdims=True)
    ln = xc * lax.rsqrt(var + EPS) * lng_ref[...] + lnb_ref[...]
    act = ln * (1.0 / (1.0 + jnp.exp(-ln)))
    br_a = jnp.dot(act.astype(BF16), pw_ref[...], preferred_element_type=F32)
    br_b = jnp.dot(og_ref[...], wgo_ref[...], preferred_element_type=F32)
    br_c = jnp.dot(od_ref[...], wdo_ref[...], preferred_element_type=F32)
    merged = (gt_ref[:, 0:D].astype(F32) * br_a + gt_ref[:, D:2 * D].astype(F32) * br_b
              + gt_ref[:, 2 * D:3 * D].astype(F32) * br_c)
    y = jnp.dot(merged.astype(BF16), wout_ref[...], preferred_element_type=F32)
    x_new = x_ref[...] + mod_ref[0, 2:3, :] * y
    xo_ref[...] = x_new
    h2 = _modnorm(x_new, g2_ref[...], mod_ref[0, 3:4, :], mod_ref[0, 4:5, :])
    h2_hi = h2.astype(BF16)
    h2_ref[...] = h2_hi
    h2_lo = (h2 - h2_hi.astype(F32)).astype(BF16)
    lg_ref[...] = (jnp.dot(h2_hi, wr_ref[0], preferred_element_type=F32)
                   + jnp.dot(h2_lo, wr_ref[0], preferred_element_type=F32)
                   + jnp.dot(h2_hi, wr_ref[1], preferred_element_type=F32)) + br_ref[...]


def _mix(u, og, od, gates, x, mod_l, dw_w, dw_b, ln_g, ln_b, pw_w, w_gqa_o, w_diff_o, w_out,
         g2, w_r, b_r, tiles_per_b, n_b):
    T = x.shape[0]
    nt = T // TM
    halo_per_tile = TM // CONV_HALO
    n_halo = T // CONV_HALO
    full = lambda shape: pl.BlockSpec(shape, lambda i: (0,) * len(shape))
    in_specs = [
        pl.BlockSpec((CONV_HALO, CONV_CH), lambda i: (jnp.maximum(i * halo_per_tile - 1, 0), 0)),
        pl.BlockSpec((TM, CONV_CH), lambda i: (i, 0)),
        pl.BlockSpec((CONV_HALO, CONV_CH),
                     lambda i: (jnp.minimum((i + 1) * halo_per_tile, n_halo - 1), 0)),
        pl.BlockSpec((TM, GQA_QH * HD), lambda i: (i, 0)),
        pl.BlockSpec((TM, DIFF_H * DIFF_V), lambda i: (i, 0)),
        pl.BlockSpec((TM, GATE_COLS), lambda i: (i, 0)),
        pl.BlockSpec((TM, D), lambda i: (i, 0)),
        pl.BlockSpec((1, 6, D), lambda i: (_mod_row(i, tiles_per_b, n_b), 0, 0)),
        full((CONV_W + 1, CONV_CH)), full((1, CONV_CH)), full((1, CONV_CH)), full((1, CONV_CH)),
        full((CONV_CH, D)), full((GQA_QH * HD, D)), full((DIFF_H * DIFF_V, D)), full((D, D)),
        full((1, D)), full((2, D, N_EXP)), full((1, N_EXP)),
    ]
    out_shape = (jax.ShapeDtypeStruct((T, D), F32), jax.ShapeDtypeStruct((T, D), BF16),
                 jax.ShapeDtypeStruct((T, N_EXP), F32))
    out_specs = (pl.BlockSpec((TM, D), lambda i: (i, 0)), pl.BlockSpec((TM, D), lambda i: (i, 0)),
                 pl.BlockSpec((TM, N_EXP), lambda i: (i, 0)))
    return pl.pallas_call(
        functools.partial(_mix_kernel, tiles_per_b=tiles_per_b),
        out_shape=out_shape, grid=(nt,), in_specs=in_specs, out_specs=out_specs,
        scratch_shapes=[pltpu.VMEM((TM + 2 * CONV_HALO, CONV_CH), F32),
                        pltpu.VMEM((SUBLANES - 1, TM + (CONV_W // SUBLANES) * SUBLANES, CONV_CH), F32)],
        compiler_params=_cparams(("arbitrary",)),
    )(u, u, u, og, od, gates, x, mod_l, dw_w, dw_b, ln_g, ln_b, pw_w, w_gqa_o, w_diff_o, w_out,
      g2, w_r, b_r)


def _expert_kernel(be_ref, nb_ref, x_ref, w1_ref, b1_ref, w2_ref, b2_ref, y_ref, w1_sc, w2_sc):
    i = pl.program_id(0)
    prev_e = be_ref[jnp.maximum(i - 1, 0)]
    new_expert = jnp.logical_or(i == 0, be_ref[i] != prev_e)

    @pl.when(new_expert)
    def _():
        w1_sc[...] = w1_ref[0].astype(BF16)
        w2_sc[...] = w2_ref[0].astype(BF16)

    @pl.when(i < nb_ref[0])
    def _():
        x = x_ref[...]
        y = jnp.zeros((EXPERT_BLOCK, D), F32) + b2_ref[0]
        for c0 in range(0, FF, FF_CHUNK):
            c1 = c0 + FF_CHUNK
            g = jnp.dot(x, w1_sc[:, c0:c1], preferred_element_type=F32) + b1_ref[0, :, c0:c1]
            li = (jnp.dot(x, w1_sc[:, FF + c0:FF + c1], preferred_element_type=F32)
                  + b1_ref[0, :, FF + c0:FF + c1])
            gate = jnp.minimum(g, SWIGLU_LIMIT)
            lin = jnp.clip(li, -SWIGLU_LIMIT, SWIGLU_LIMIT)
            act = (lin + 1.0) * (gate * (1.0 / (1.0 + jnp.exp(-gate * SWIGLU_ALPHA))))
            y = y + jnp.dot(act.astype(BF16), w2_sc[c0:c1, :], preferred_element_type=F32)
        y_ref[...] = y.astype(y_ref.dtype)

    @pl.when(i >= nb_ref[0])
    def _():
        y_ref[...] = jnp.zeros(y_ref.shape, y_ref.dtype)


def _experts(xb, blk_e, n_used, w1, b1, w2, b2, l):
    n_slots = xb.shape[0]
    n_blocks = n_slots // EXPERT_BLOCK
    L = w1.shape[0]
    grid_spec = pltpu.PrefetchScalarGridSpec(
        num_scalar_prefetch=2, grid=(n_blocks,),
        in_specs=[
            pl.BlockSpec((EXPERT_BLOCK, D), lambda i, be, nb: (i, 0)),
            pl.BlockSpec((None, 1, D, 2 * FF), lambda i, be, nb: (l, be[i], 0, 0)),
            pl.BlockSpec((None, 1, 1, 2 * FF), lambda i, be, nb: (l, be[i], 0, 0)),
            pl.BlockSpec((None, 1, FF, D), lambda i, be, nb: (l, be[i], 0, 0)),
            pl.BlockSpec((None, 1, 1, D), lambda i, be, nb: (l, be[i], 0, 0)),
        ],
        out_specs=pl.BlockSpec((EXPERT_BLOCK, D), lambda i, be, nb: (i, 0)),
        scratch_shapes=[pltpu.VMEM((D, 2 * FF), BF16), pltpu.VMEM((FF, D), BF16)],
    )
    return pl.pallas_call(
        _expert_kernel, grid_spec=grid_spec,
        out_shape=jax.ShapeDtypeStruct((n_slots, D), BF16),
        compiler_params=_cparams(("arbitrary",)),
    )(blk_e, n_used, xb, w1, b1.reshape(L, N_EXP, 1, 2 * FF), w2, b2.reshape(L, N_EXP, 1, D))


def _combine_kernel(x_ref, yg_ref, cw_ref, mod_ref, o_ref):
    cw = cw_ref[...]
    f = jnp.zeros((TM, D), F32)
    for k in range(TOP_K):
        f = f + cw[:, k:k + 1] * yg_ref[k].astype(F32)
    o_ref[...] = x_ref[...] + mod_ref[0, 5:6, :] * f


def _combine(x, yg, comb, mod_l, tiles_per_b, n_b):
    T = x.shape[0]
    return pl.pallas_call(
        _combine_kernel,
        out_shape=jax.ShapeDtypeStruct((T, D), F32),
        grid=(T // TM,),
        in_specs=[pl.BlockSpec((TM, D), lambda i: (i, 0)),
                  pl.BlockSpec((TOP_K, TM, D), lambda i: (0, i, 0)),
                  pl.BlockSpec((TM, TOP_K), lambda i: (i, 0)),
                  pl.BlockSpec((1, 6, D), lambda i: (_mod_row(i, tiles_per_b, n_b), 0, 0))],
        out_specs=pl.BlockSpec((TM, D), lambda i: (i, 0)),
        compiler_params=_cparams(("arbitrary",)),
    )(x, yg, comb, mod_l)


def _router_kernel(lg_ref, idx_ref, comb_ref, rank_ref, cnt_ref, cnt_sc):
    i = pl.program_id(0)

    @pl.when(i == 0)
    def _():
        cnt_sc[...] = jnp.zeros(cnt_sc.shape, F32)

    x = lg_ref[...]
    lane = lax.broadcasted_iota(jnp.int32, (TM, N_EXP), 1)
    col = lax.broadcasted_iota(jnp.int32, (TM, TOP_K), 1)
    vals, hots = [], []
    idx_out = jnp.zeros((TM, TOP_K), jnp.int32)
    for k in range(TOP_K):
        mx = jnp.max(x, axis=-1, keepdims=True)
        idx = jnp.min(jnp.where(x == mx, lane, N_EXP), axis=-1, keepdims=True)
        hot = lane == idx
        vals.append(mx)
        hots.append(hot)
        idx_out = jnp.where(col == k, idx, idx_out)
        x = jnp.where(hot, -jnp.inf, x)
    idx_ref[...] = idx_out

    es = [jnp.exp(v - vals[0]) for v in vals]
    denom = es[0] + es[1] + es[2] + es[3]
    comb = jnp.zeros((TM, TOP_K), F32)
    for k in range(TOP_K):
        comb = jnp.where(col == k, es[k] / denom, comb)
    comb_ref[...] = comb

    hot_sum = jnp.zeros((TM, N_EXP), F32)
    for k in range(TOP_K):
        hot_sum = hot_sum + jnp.where(hots[k], 1.0, 0.0)
    r_i = lax.broadcasted_iota(jnp.int32, (TM, TM), 0)
    c_i = lax.broadcasted_iota(jnp.int32, (TM, TM), 1)
    lower = jnp.where(c_i < r_i, 1.0, 0.0).astype(BF16)
    before = jnp.dot(lower, hot_sum.astype(BF16), preferred_element_type=F32) + cnt_sc[...]
    rank = jnp.zeros((TM, TOP_K), jnp.int32)
    for k in range(TOP_K):
        rk = jnp.sum(jnp.where(hots[k], before, 0.0), axis=-1, keepdims=True)
        rank = jnp.where(col == k, rk.astype(jnp.int32), rank)
    rank_ref[...] = rank
    cnt = cnt_sc[...] + jnp.sum(hot_sum, axis=0, keepdims=True)
    cnt_sc[...] = cnt
    cnt_ref[...] = cnt.astype(jnp.int32)


def _router(logits):
    T = logits.shape[0]
    tok = lambda w: pl.BlockSpec((TM, w), lambda i: (i, 0))
    return pl.pallas_call(
        _router_kernel,
        out_shape=(jax.ShapeDtypeStruct((T, TOP_K), jnp.int32), jax.ShapeDtypeStruct((T, TOP_K), F32),
                   jax.ShapeDtypeStruct((T, TOP_K), jnp.int32), jax.ShapeDtypeStruct((1, N_EXP), jnp.int32)),
        grid=(T // TM,),
        in_specs=[tok(N_EXP)],
        out_specs=(tok(TOP_K), tok(TOP_K), tok(TOP_K), pl.BlockSpec((1, N_EXP), lambda i: (0, 0))),
        scratch_shapes=[pltpu.VMEM((1, N_EXP), F32)],
        compiler_params=_cparams(("arbitrary",)),
    )(logits)


def _slot_tables(top_idx, rank, counts):
    t = top_idx.shape[0]
    n_assign = t * TOP_K
    counts = counts.reshape(N_EXP)
    padded = (counts + EXPERT_BLOCK - 1) // EXPERT_BLOCK * EXPERT_BLOCK
    pad_end = jnp.cumsum(padded)
    pad_start = pad_end - padded
    hot = top_idx[:, :, None] == jnp.arange(N_EXP, dtype=jnp.int32)[None, None, :]
    slot_of = jnp.sum(jnp.where(hot, pad_start[None, None, :], 0), axis=-1) + rank
    n_blocks = -(-n_assign // EXPERT_BLOCK) + N_EXP
    n_slots = n_blocks * EXPERT_BLOCK
    tok = jnp.broadcast_to(jnp.arange(t, dtype=jnp.int32)[:, None], (t, TOP_K))
    slot_tok = jnp.full((n_slots,), t, jnp.int32).at[slot_of.reshape(-1)].set(tok.reshape(-1))
    blk_start = jnp.arange(n_blocks, dtype=jnp.int32) * EXPERT_BLOCK
    blk_e = jnp.minimum(jnp.sum((pad_end[None, :] <= blk_start[:, None]).astype(jnp.int32), axis=1),
                        N_EXP - 1).astype(jnp.int32)
    n_used = (pad_end[-1] // EXPERT_BLOCK).astype(jnp.int32).reshape(1)
    return slot_tok, slot_of.T.reshape(-1).astype(jnp.int32), blk_e, n_used


def _rope_tables(tiles_per_b):
    n_lat = (tiles_per_b - 1) * TM
    pos = jnp.arange(n_lat, dtype=F32)
    t_row = jnp.floor(pos / GRID_W)
    t_col = pos - t_row * GRID_W
    n_freq = HD // 4
    inv_freq = ROPE_THETA ** (-jnp.arange(n_freq, dtype=F32) / n_freq)
    lane = jnp.arange(LANES)
    d = lane % HD
    use_col = (d // 32) == 1
    f = d % n_freq
    first = (d % 32) < n_freq
    ang = jnp.where(use_col[None, :], t_col[:, None], t_row[:, None]) * inv_freq[f][None, :]
    cos = jnp.cos(ang)
    sin = jnp.where(first[None, :], -jnp.sin(ang), jnp.sin(ang))
    cos = jnp.concatenate([jnp.ones((TM, LANES), F32), cos], axis=0)
    sin = jnp.concatenate([jnp.zeros((TM, LANES), F32), sin], axis=0)
    return cos, sin


def _permute_w_in(w):
    gk, gv, dk, dv = w[:, 0:128], w[:, 128:256], w[:, 256:768], w[:, 768:1280]
    return jnp.concatenate([gk, dk, w[:, 1280:2304], gv, dv, w[:, 2304:]], axis=1).astype(BF16)


def kernel(x, c, ctx, c_ctx, w_mod, b_mod, norm1_g, norm2_g, w_in, gqa_q_norm, gqa_k_norm, diff_q_norm, diff_k_norm, lam_q1, lam_k1, lam_q2, lam_k2, diff_subln_g, conv_dw_w, conv_dw_b, conv_ln_g, conv_ln_b, conv_pw_w, w_gqa_o, w_diff_o, b_gate, w_out, router_w, router_b, exp_w1, exp_b1, exp_w2, exp_b2):
    n_b, n_lat, d = x.shape
    n_ctx = ctx.shape[1]
    L = w_mod.shape[0]
    assert d == D and n_ctx == TM and n_lat % TM == 0 and n_b < 8
    tiles_per_b = (n_ctx + n_lat) // TM
    T = n_b * tiles_per_b * TM

    xs = jnp.concatenate([ctx, x], axis=1).reshape(T, D)
    cc = jnp.zeros((8, D), F32).at[:n_b].set(c).at[n_b].set(c_ctx)
    mod = _modulation(cc, w_mod, b_mod).reshape(L, 8, 6, D)
    cos_t, sin_t = _rope_tables(tiles_per_b)
    q_scale = HD ** -0.5 * math.log2(math.e)

    for l in range(L):
        lam_init = 0.8 - 0.6 * math.exp(-0.3 * l)
        mod_l = mod[l]
        tile2 = lambda g: jnp.tile(g, 2)
        qk_gain = jnp.concatenate(
            [tile2(gqa_k_norm[l])] * (GQA_KVH // 2) + [tile2(diff_k_norm[l])] * DIFF_H
            + [tile2(gqa_q_norm[l]) * q_scale] * (GQA_QH // 2)
            + [tile2(diff_q_norm[l]) * q_scale] * DIFF_H).reshape(1, QK_COLS)
        kg, kd, qt, vtg, vtd, u, gates = _inproj(
            xs, mod_l, norm1_g[l].reshape(1, D), _permute_w_in(w_in[l]), cos_t, sin_t, qk_gain,
            b_gate[l].reshape(1, GATE_COLS), tiles_per_b, n_b)
        og = _attention(qt, kg, vtg, n_b, tiles_per_b, diff=False)
        od = _attention(qt, kd, vtd, n_b, tiles_per_b, diff=True,
                        lam_refs=[v[l].reshape(1, HD) for v in (lam_q1, lam_k1, lam_q2, lam_k2)],
                        subln_g=diff_subln_g[l].reshape(1, DIFF_V), lam_init=lam_init)
        dw_w = jnp.concatenate([conv_dw_w[l], jnp.zeros((1, CONV_CH), F32)], axis=0)
        w_r_hi = router_w[l].astype(BF16)
        w_r = jnp.stack([w_r_hi, (router_w[l] - w_r_hi.astype(F32)).astype(BF16)])
        xs, h2, logits = _mix(
            u, og, od, gates, xs, mod_l, dw_w, conv_dw_b[l].reshape(1, CONV_CH),
            conv_ln_g[l].reshape(1, CONV_CH), conv_ln_b[l].reshape(1, CONV_CH),
            conv_pw_w[l].astype(BF16), w_gqa_o[l].astype(BF16), w_diff_o[l].astype(BF16),
            w_out[l].astype(BF16), norm2_g[l].reshape(1, D), w_r,
            router_b[l].reshape(1, N_EXP), tiles_per_b, n_b)

        top_idx, comb, rank, counts = _router(logits)
        slot_tok, slot_of, blk_e, n_used = _slot_tables(top_idx, rank, counts)
        h2_pad = jnp.concatenate([h2, jnp.zeros((1, D), BF16)], axis=0)
        xb = h2_pad[slot_tok]
        yb = _experts(xb, blk_e, n_used, exp_w1, exp_b1, exp_w2, exp_b2, l)
        yg = yb[slot_of].reshape(TOP_K, T, D)
        xs = _combine(xs, yg, comb, mod_l, tiles_per_b, n_b)

    return xs.reshape(n_b, n_ctx + n_lat, D)[:, n_ctx:, :]
```

```python
import functools
import math

import jax
import jax.numpy as jnp
from jax import lax
from jax.experimental import pallas as pl
from jax.experimental.pallas import tpu as pltpu

F32 = jnp.float32
BF16 = jnp.bfloat16

D = 1024
HD = 64
GRID_W = 64
ROPE_THETA = 10000.0
EPS = 1e-6
CONV_CH = 512
CONV_W = 31
CONV_HALO = 16
GQA_QH = 8
GQA_KVH = 2
DIFF_H = 4
DIFF_V = 128
N_EXP = 32
TOP_K = 4
FF = 1024
SWIGLU_ALPHA = 1.702
SWIGLU_LIMIT = 7.0
EXPERT_BLOCK = 256
FF_CHUNK = 1024
TM = 256
LANES = 128
SUBLANES = 8
SUM_ROWS = 16
N_STREAMS = 1
GQA_LAG = 6
DIFF_LAG = 5
UNROLL = 8

N_KHEADS = GQA_KVH + 2 * DIFF_H
N_QHEADS = GQA_QH + 2 * DIFF_H
QK_COLS = (N_KHEADS + N_QHEADS) * HD
V_COLS = GQA_KVH * HD + DIFF_H * DIFF_V
GATE_COLS = 3 * D
D_IN = QK_COLS + V_COLS + 2 * CONV_CH + GATE_COLS
VMEM_LIMIT = 56 * 1024 * 1024


def _cparams(sem):
    return pltpu.CompilerParams(dimension_semantics=sem, vmem_limit_bytes=VMEM_LIMIT)


def _mod_kernel(c_ref, w_ref, b_ref, o_ref):
    c = c_ref[...]
    s = c * (1.0 / (1.0 + jnp.exp(-c)))
    o_ref[0] = jnp.dot(s, w_ref[0], preferred_element_type=F32,
                       precision=lax.Precision.HIGHEST) + b_ref[0]


def _modulation(cc, w_mod, b_mod):
    L = w_mod.shape[0]
    R = cc.shape[0]
    tn = 1536
    return pl.pallas_call(
        _mod_kernel,
        out_shape=jax.ShapeDtypeStruct((L, R, 6 * D), F32),
        grid=(L, 6 * D // tn),
        in_specs=[pl.BlockSpec((R, D), lambda l, j: (0, 0)),
                  pl.BlockSpec((1, D, tn), lambda l, j: (l, 0, j)),
                  pl.BlockSpec((1, 1, tn), lambda l, j: (l, 0, j))],
        out_specs=pl.BlockSpec((1, R, tn), lambda l, j: (l, 0, j)),
        compiler_params=_cparams(("arbitrary", "arbitrary")),
    )(cc, w_mod, b_mod.reshape(L, 1, 6 * D))


def _modnorm(x, g, shift, scale):
    ms = jnp.mean(x * x, axis=-1, keepdims=True)
    return (x * lax.rsqrt(ms + EPS) * g) * (1.0 + scale) + shift


def _inproj_kernel(x_ref, mod_ref, g1_ref, w_ref, cos_ref, sin_ref, qkg_ref, bg_ref,
                   kg_out, kd_out, qt_out, vtg_out, vtd_out, u_out, g_out):
    h = _modnorm(x_ref[...], g1_ref[...], mod_ref[0, 0:1, :], mod_ref[0, 1:2, :]).astype(BF16)
    cos = cos_ref[...]
    sin = sin_ref[...]
    lane = lax.broadcasted_iota(jnp.int32, (1, LANES), 1)
    first = (lane % 32) < 16
    r_i = lax.broadcasted_iota(jnp.int32, (LANES, LANES), 0) // HD
    c_i = lax.broadcasted_iota(jnp.int32, (LANES, LANES), 1) // HD
    head_ones = jnp.where(r_i == c_i, 1.0, 0.0).astype(BF16)

    def proj(c0, c1):
        return jnp.dot(h, w_ref[:, c0:c1], preferred_element_type=F32)

    n_pairs = QK_COLS // LANES
    pairs_per_dot = 4
    for p0 in range(0, n_pairs, pairs_per_dot):
        p1 = min(p0 + pairs_per_dot, n_pairs)
        z = proj(p0 * LANES, p1 * LANES)
        for p in range(p0, p1):
            zc = z[:, (p - p0) * LANES:(p - p0 + 1) * LANES]
            ss = jnp.dot((zc * zc).astype(BF16), head_ones, preferred_element_type=F32)
            n = zc * lax.rsqrt(ss * (1.0 / HD) + EPS) * qkg_ref[:, p * LANES:(p + 1) * LANES]
            partner = jnp.where(first, pltpu.roll(n, LANES - 16, 1), pltpu.roll(n, 16, 1))
            r = n * cos + partner * sin
            if 2 * p < GQA_KVH:
                kg_out[2 * p] = r[:, :HD].astype(BF16)
                kg_out[2 * p + 1] = r[:, HD:].astype(BF16)
            elif 2 * p < N_KHEADS:
                kd_out[2 * p - GQA_KVH] = r[:, :HD].astype(BF16)
                kd_out[2 * p - GQA_KVH + 1] = r[:, HD:].astype(BF16)
            else:
                rt = r.T
                qh = 2 * p - N_KHEADS
                qt_out[qh] = rt[:HD, :].astype(BF16)
                qt_out[qh + 1] = rt[HD:, :].astype(BF16)

    z = proj(QK_COLS, QK_COLS + V_COLS)
    gv_pairs = GQA_KVH * HD // LANES
    for p in range(V_COLS // LANES):
        zt = z[:, p * LANES:(p + 1) * LANES].T.astype(BF16)
        if p < gv_pairs:
            vtg_out[0, p * LANES:(p + 1) * LANES, :] = zt
        else:
            vtd_out[0, (p - gv_pairs) * LANES:(p - gv_pairs + 1) * LANES, :] = zt

    c0 = QK_COLS + V_COLS
    a = proj(c0, c0 + 2 * CONV_CH)
    gl = a[:, CONV_CH:]
    u_out[...] = a[:, :CONV_CH] * (1.0 / (1.0 + jnp.exp(-gl)))

    c0 = c0 + 2 * CONV_CH
    for j in range(3):
        gp = proj(c0 + j * D, c0 + (j + 1) * D) + bg_ref[:, j * D:(j + 1) * D]
        g_out[:, j * D:(j + 1) * D] = (1.0 / (1.0 + jnp.exp(-gp))).astype(BF16)


def _mod_row(i, tiles_per_b, n_b):
    return jnp.where(i % tiles_per_b == 0, n_b, i // tiles_per_b)


def _inproj(x, mod_l, g1, w_perm, cos_t, sin_t, qk_gain, b_gate, tiles_per_b, n_b):
    T = x.shape[0]
    nt = T // TM
    out_shape = (
        jax.ShapeDtypeStruct((GQA_KVH, T, HD), BF16),
        jax.ShapeDtypeStruct((2 * DIFF_H, T, HD), BF16),
        jax.ShapeDtypeStruct((N_QHEADS, HD, T), BF16),
        jax.ShapeDtypeStruct((nt, GQA_KVH * HD, TM), BF16),
        jax.ShapeDtypeStruct((nt, DIFF_H * DIFF_V, TM), BF16),
        jax.ShapeDtypeStruct((T, CONV_CH), F32),
        jax.ShapeDtypeStruct((T, GATE_COLS), BF16),
    )
    in_specs = [
        pl.BlockSpec((TM, D), lambda i: (i, 0)),
        pl.BlockSpec((1, 6, D), lambda i: (_mod_row(i, tiles_per_b, n_b), 0, 0)),
        pl.BlockSpec((1, D), lambda i: (0, 0)),
        pl.BlockSpec((D, D_IN), lambda i: (0, 0)),
        pl.BlockSpec((TM, LANES), lambda i: (i % tiles_per_b, 0)),
        pl.BlockSpec((TM, LANES), lambda i: (i % tiles_per_b, 0)),
        pl.BlockSpec((1, QK_COLS), lambda i: (0, 0)),
        pl.BlockSpec((1, GATE_COLS), lambda i: (0, 0)),
    ]
    out_specs = (
        pl.BlockSpec((GQA_KVH, TM, HD), lambda i: (0, i, 0)),
        pl.BlockSpec((2 * DIFF_H, TM, HD), lambda i: (0, i, 0)),
        pl.BlockSpec((N_QHEADS, HD, TM), lambda i: (0, 0, i)),
        pl.BlockSpec((1, GQA_KVH * HD, TM), lambda i: (i, 0, 0)),
        pl.BlockSpec((1, DIFF_H * DIFF_V, TM), lambda i: (i, 0, 0)),
        pl.BlockSpec((TM, CONV_CH), lambda i: (i, 0)),
        pl.BlockSpec((TM, GATE_COLS), lambda i: (i, 0)),
    )
    return pl.pallas_call(
        _inproj_kernel, out_shape=out_shape, grid=(nt,),
        in_specs=in_specs, out_specs=out_specs,
        compiler_params=_cparams(("arbitrary",)),
    )(x, mod_l, g1, w_perm, cos_t, sin_t, qk_gain, b_gate)


def _attn_kernel(*refs, n_maps, n_v, share_k, dv, diff, lam_init):
    maps_per_v = n_maps // n_v
    if diff:
        (qt_ref, k_ref, vt_ref, lq1, lk1, lq2, lk2, sg_ref, o_ref, s0_sc, s1_sc, m_sc, acc_sc) = refs
    else:
        (qt_ref, k_ref, vt_ref, o_ref, s0_sc, s1_sc, m_sc, acc_sc) = refs
    qi = pl.program_id(2)
    n_iter = jnp.where(qi == 0, 0, (vt_ref.shape[0] - 1) // UNROLL)

    m_sc[...] = jnp.full(m_sc.shape, -jnp.inf, F32)
    acc_sc[...] = jnp.zeros(acc_sc.shape, F32)
    ones = jnp.ones((SUM_ROWS, TM), BF16)

    def score(j, s_sc, m):
        off = pl.multiple_of(j * TM, TM)
        kc = k_ref[0 if share_k else m, pl.ds(off, TM), :]
        s_sc[m] = jnp.dot(kc, qt_ref[m], preferred_element_type=F32)

    def softmax_pv(j, s_sc, m):
        h = m // maps_per_v
        vt = jnp.concatenate([vt_ref[j, h * dv:(h + 1) * dv, :], ones], axis=0)
        s = s_sc[m]
        m_prev = m_sc[m]
        m_new = jnp.maximum(m_prev, jnp.max(s, axis=0, keepdims=True))
        p = jnp.exp2(s - m_new).astype(BF16)
        alpha = jnp.exp2(m_prev - m_new)
        acc_sc[m] = alpha * acc_sc[m] + jnp.dot(vt, p, preferred_element_type=F32)
        m_sc[m] = m_new

    LAG = DIFF_LAG if diff else GQA_LAG
    bufs = (s0_sc, s1_sc)
    last_chunk = vt_ref.shape[0] - 1

    def score_pos(base_chunk, pp):
        c, m = divmod(pp, n_maps)
        score(jnp.minimum(base_chunk + c, last_chunk), bufs[c % 2], m)

    def pv_pos(base_chunk, pp):
        c, m = divmod(pp, n_maps)
        softmax_pv(base_chunk + c, bufs[c % 2], m)

    for pp in range(LAG):
        score_pos(0, pp)

    def body(i, carry):
        j = UNROLL * i
        for pp in range(UNROLL * n_maps):
            score_pos(j, pp + LAG)
            pv_pos(j, pp)
        return carry

    lax.fori_loop(0, n_iter, body, 0)
    for pp in range(n_maps):
        if pp + LAG < n_maps:
            score_pos(UNROLL * n_iter, pp + LAG)
        pv_pos(UNROLL * n_iter, pp)

    def out(m):
        return acc_sc[m, 0:dv, :] / acc_sc[m, dv:dv + 1, :]

    if diff:
        lam = (jnp.exp(jnp.sum(lq1[...] * lk1[...], axis=-1, keepdims=True))
               - jnp.exp(jnp.sum(lq2[...] * lk2[...], axis=-1, keepdims=True)) + lam_init)
        for h in range(n_v):
            o = (out(2 * h) - lam * out(2 * h + 1)).T
            ms = jnp.mean(o * o, axis=-1, keepdims=True)
            o_ref[:, h * dv:(h + 1) * dv] = (o * lax.rsqrt(ms + EPS) * sg_ref[...]
                                             * (1.0 - lam_init)).astype(o_ref.dtype)
    else:
        o = jnp.concatenate([out(m) for m in range(n_maps)], axis=0)
        o_ref[...] = o.T.astype(o_ref.dtype)


def _attention(qt, k3, vt3, n_b, tiles_per_b, *, diff, lam_refs=None, subln_g=None, lam_init=0.0):
    T = qt.shape[2]
    n_maps = 4
    if diff:
        n_groups, n_v, share_k, dv = DIFF_H // 2, 2, False, DIFF_V
        q_unit0 = GQA_QH // n_maps
        k_blk = n_maps
    else:
        n_groups, n_v, share_k, dv = GQA_KVH, 1, True, HD
        q_unit0 = 0
        k_blk = 1
    vt4 = vt3.reshape(n_b, tiles_per_b, vt3.shape[1], TM)
    tokens_per_b = tiles_per_b * TM
    in_specs = [
        pl.BlockSpec((n_maps, HD, TM), lambda b, g, q: (q_unit0 + g, 0, b * tiles_per_b + q)),
        pl.BlockSpec((k_blk, tokens_per_b, HD), lambda b, g, q: (g, b, 0)),
        pl.BlockSpec((None, tiles_per_b, n_v * dv, TM), lambda b, g, q: (b, 0, g, 0)),
    ]
    args = [qt, k3, vt4]
    if diff:
        in_specs += [pl.BlockSpec((1, HD), lambda b, g, q: (0, 0))] * 4
        in_specs += [pl.BlockSpec((1, DIFF_V), lambda b, g, q: (0, 0))]
        args += list(lam_refs) + [subln_g]
    out_cols = n_v * dv if diff else n_maps * dv
    kern = functools.partial(_attn_kernel, n_maps=n_maps, n_v=n_v, share_k=share_k, dv=dv, diff=diff,
                             lam_init=lam_init)
    return pl.pallas_call(
        kern,
        out_shape=jax.ShapeDtypeStruct((T, n_groups * out_cols), BF16),
        grid=(n_b, n_groups, tiles_per_b),
        in_specs=in_specs,
        out_specs=pl.BlockSpec((TM, out_cols), lambda b, g, q: (b * tiles_per_b + q, g)),
        scratch_shapes=[pltpu.VMEM((n_maps, TM, TM), F32), pltpu.VMEM((n_maps, TM, TM), F32),
                        pltpu.VMEM((n_maps, 1, TM), F32),
                        pltpu.VMEM((n_maps, dv + SUM_ROWS, TM), F32)],
        compiler_params=_cparams(("arbitrary", "arbitrary", "arbitrary")),
    )(*args)


def _mix_kernel(up_ref, uc_ref, un_ref, og_ref, od_ref, gt_ref, x_ref, mod_ref,
                dww_ref, dwb_ref, lng_ref, lnb_ref, pw_ref, wgo_ref, wdo_ref, wout_ref,
                g2_ref, wr_ref, br_ref,
                xo_ref, h2_ref, lg_ref, ext_sc, sh_sc, *, tiles_per_b):
    i = pl.program_id(0)
    ti = i % tiles_per_b
    has_prev = ti > 1
    has_next = jnp.logical_and(ti > 0, ti < tiles_per_b - 1)
    ext_sc[0:CONV_HALO, :] = jnp.where(has_prev, up_ref[...], 0.0)
    ext_sc[CONV_HALO:CONV_HALO + TM, :] = uc_ref[...]
    ext_sc[CONV_HALO + TM:, :] = jnp.where(has_next, un_ref[...], 0.0)
    acc = jnp.zeros((TM, CONV_CH), F32) + dwb_ref[...]
    base = CONV_HALO - CONV_W // 2
    span = TM + (CONV_W // SUBLANES) * SUBLANES
    for r in range(SUBLANES):
        taps = [w for w in range(CONV_W) if (base + w) % SUBLANES == r]
        if not taps:
            continue
        if r == 0:
            src = ext_sc
        else:
            sh_sc[r - 1] = ext_sc[r:r + span, :]
            src = sh_sc.at[r - 1]
        for w in taps:
            a = (base + w) // SUBLANES * SUBLANES
            acc = acc + src[a:a + TM, :] * dww_ref[w:w + 1, :]
    mu = jnp.mean(acc, axis=-1, keepdims=True)
    xc = acc - mu
    var = jnp.mean(xc * xc, axis=-1, keepdims=True)
    ln = xc * lax.rsqrt(var + EPS) * lng_ref[...] + lnb_ref[...]
    act = ln * (1.0 / (1.0 + jnp.exp(-ln)))
    br_a = jnp.dot(act.astype(BF16), pw_ref[...], preferred_element_type=F32)
    br_b = jnp.dot(og_ref[...], wgo_ref[...], preferred_element_type=F32)
    br_c = jnp.dot(od_ref[...], wdo_ref[...], preferred_element_type=F32)
    merged = (gt_ref[:, 0:D].astype(F32) * br_a + gt_ref[:, D:2 * D].astype(F32) * br_b
              + gt_ref[:, 2 * D:3 * D].astype(F32) * br_c)
    y = jnp.dot(merged.astype(BF16), wout_ref[...], preferred_element_type=F32)
    x_new = x_ref[...] + mod_ref[0, 2:3, :] * y
    xo_ref[...] = x_new
    h2 = _modnorm(x_new, g2_ref[...], mod_ref[0, 3:4, :], mod_ref[0, 4:5, :])
    h2_hi = h2.astype(BF16)
    h2_ref[...] = h2_hi
    h2_lo = (h2 - h2_hi.astype(F32)).astype(BF16)
    lg_ref[...] = (jnp.dot(h2_hi, wr_ref[0], preferred_element_type=F32)
                   + jnp.dot(h2_lo, wr_ref[0], preferred_element_type=F32)
                   + jnp.dot(h2_hi, wr_ref[1], preferred_element_type=F32)) + br_ref[...]


def _mix(u, og, od, gates, x, mod_l, dw_w, dw_b, ln_g, ln_b, pw_w, w_gqa_o, w_diff_o, w_out,
         g2, w_r, b_r, tiles_per_b, n_b):
    T = x.shape[0]
    nt = T // TM
    halo_per_tile = TM // CONV_HALO
    n_halo = T // CONV_HALO
    full = lambda shape: pl.BlockSpec(shape, lambda i: (0,) * len(shape))
    in_specs = [
        pl.BlockSpec((CONV_HALO, CONV_CH), lambda i: (jnp.maximum(i * halo_per_tile - 1, 0), 0)),
        pl.BlockSpec((TM, CONV_CH), lambda i: (i, 0)),
        pl.BlockSpec((CONV_HALO, CONV_CH),
                     lambda i: (jnp.minimum((i + 1) * halo_per_tile, n_halo - 1), 0)),
        pl.BlockSpec((TM, GQA_QH * HD), lambda i: (i, 0)),
        pl.BlockSpec((TM, DIFF_H * DIFF_V), lambda i: (i, 0)),
        pl.BlockSpec((TM, GATE_COLS), lambda i: (i, 0)),
        pl.BlockSpec((TM, D), lambda i: (i, 0)),
        pl.BlockSpec((1, 6, D), lambda i: (_mod_row(i, tiles_per_b, n_b), 0, 0)),
        full((CONV_W + 1, CONV_CH)), full((1, CONV_CH)), full((1, CONV_CH)), full((1, CONV_CH)),
        full((CONV_CH, D)), full((GQA_QH * HD, D)), full((DIFF_H * DIFF_V, D)), full((D, D)),
        full((1, D)), full((2, D, N_EXP)), full((1, N_EXP)),
    ]
    out_shape = (jax.ShapeDtypeStruct((T, D), F32), jax.ShapeDtypeStruct((T, D), BF16),
                 jax.ShapeDtypeStruct((T, N_EXP), F32))
    out_specs = (pl.BlockSpec((TM, D), lambda i: (i, 0)), pl.BlockSpec((TM, D), lambda i: (i, 0)),
                 pl.BlockSpec((TM, N_EXP), lambda i: (i, 0)))
    return pl.pallas_call(
        functools.partial(_mix_kernel, tiles_per_b=tiles_per_b),
        out_shape=out_shape, grid=(nt,), in_specs=in_specs, out_specs=out_specs,
        scratch_shapes=[pltpu.VMEM((TM + 2 * CONV_HALO, CONV_CH), F32),
                        pltpu.VMEM((SUBLANES - 1, TM + (CONV_W // SUBLANES) * SUBLANES, CONV_CH), F32)],
        compiler_params=_cparams(("arbitrary",)),
    )(u, u, u, og, od, gates, x, mod_l, dw_w, dw_b, ln_g, ln_b, pw_w, w_gqa_o, w_diff_o, w_out,
      g2, w_r, b_r)


def _expert_kernel(be_ref, nb_ref, x_ref, w1_ref, b1_ref, w2_ref, b2_ref, y_ref, w1_sc, w2_sc):
    i = pl.program_id(0)
    prev_e = be_ref[jnp.maximum(i - 1, 0)]
    new_expert = jnp.logical_or(i == 0, be_ref[i] != prev_e)

    @pl.when(new_expert)
    def _():
        w1_sc[...] = w1_ref[0].astype(BF16)
        w2_sc[...] = w2_ref[0].astype(BF16)

    @pl.when(i < nb_ref[0])
    def _():
        x = x_ref[...]
        y = jnp.zeros((EXPERT_BLOCK, D), F32) + b2_ref[0]
        for c0 in range(0, FF, FF_CHUNK):
            c1 = c0 + FF_CHUNK
            g = jnp.dot(x, w1_sc[:, c0:c1], preferred_element_type=F32) + b1_ref[0, :, c0:c1]
            li = (jnp.dot(x, w1_sc[:, FF + c0:FF + c1], preferred_element_type=F32)
                  + b1_ref[0, :, FF + c0:FF + c1])
            gate = jnp.minimum(g, SWIGLU_LIMIT)
            lin = jnp.clip(li, -SWIGLU_LIMIT, SWIGLU_LIMIT)
            act = (lin + 1.0) * (gate * (1.0 / (1.0 + jnp.exp(-gate * SWIGLU_ALPHA))))
            y = y + jnp.dot(act.astype(BF16), w2_sc[c0:c1, :], preferred_element_type=F32)
        y_ref[...] = y.astype(y_ref.dtype)

    @pl.when(i >= nb_ref[0])
    def _():
        y_ref[...] = jnp.zeros(y_ref.shape, y_ref.dtype)


def _experts(xb, blk_e, n_used, w1, b1, w2, b2, l):
    n_slots = xb.shape[0]
    n_blocks = n_slots // EXPERT_BLOCK
    L = w1.shape[0]
    grid_spec = pltpu.PrefetchScalarGridSpec(
        num_scalar_prefetch=2, grid=(n_blocks,),
        in_specs=[
            pl.BlockSpec((EXPERT_BLOCK, D), lambda i, be, nb: (i, 0)),
            pl.BlockSpec((None, 1, D, 2 * FF), lambda i, be, nb: (l, be[i], 0, 0)),
            pl.BlockSpec((None, 1, 1, 2 * FF), lambda i, be, nb: (l, be[i], 0, 0)),
            pl.BlockSpec((None, 1, FF, D), lambda i, be, nb: (l, be[i], 0, 0)),
            pl.BlockSpec((None, 1, 1, D), lambda i, be, nb: (l, be[i], 0, 0)),
        ],
        out_specs=pl.BlockSpec((EXPERT_BLOCK, D), lambda i, be, nb: (i, 0)),
        scratch_shapes=[pltpu.VMEM((D, 2 * FF), BF16), pltpu.VMEM((FF, D), BF16)],
    )
    return pl.pallas_call(
        _expert_kernel, grid_spec=grid_spec,
        out_shape=jax.ShapeDtypeStruct((n_slots, D), BF16),
        compiler_params=_cparams(("arbitrary",)),
    )(blk_e, n_used, xb, w1, b1.reshape(L, N_EXP, 1, 2 * FF), w2, b2.reshape(L, N_EXP, 1, D))


def _combine_kernel(x_ref, yg_ref, cw_ref, mod_ref, o_ref):
    cw = cw_ref[...]
    f = jnp.zeros((TM, D), F32)
    for k in range(TOP_K):
        f = f + cw[:, k:k + 1] * yg_ref[k].astype(F32)
    o_ref[...] = x_ref[...] + mod_ref[0, 5:6, :] * f


def _combine(x, yg, comb, mod_l, tiles_per_b, n_b):
    T = x.shape[0]
    return pl.pallas_call(
        _combine_kernel,
        out_shape=jax.ShapeDtypeStruct((T, D), F32),
        grid=(T // TM,),
        in_specs=[pl.BlockSpec((TM, D), lambda i: (i, 0)),
                  pl.BlockSpec((TOP_K, TM, D), lambda i: (0, i, 0)),
                  pl.BlockSpec((TM, TOP_K), lambda i: (i, 0)),
                  pl.BlockSpec((1, 6, D), lambda i: (_mod_row(i, tiles_per_b, n_b), 0, 0))],
        out_specs=pl.BlockSpec((TM, D), lambda i: (i, 0)),
        compiler_params=_cparams(("arbitrary",)),
    )(x, yg, comb, mod_l)


def _router_kernel(lg_ref, idx_ref, comb_ref, rank_ref, cnt_ref, cnt_sc):
    i = pl.program_id(0)

    @pl.when(i == 0)
    def _():
        cnt_sc[...] = jnp.zeros(cnt_sc.shape, F32)

    x = lg_ref[...]
    lane = lax.broadcasted_iota(jnp.int32, (TM, N_EXP), 1)
    col = lax.broadcasted_iota(jnp.int32, (TM, TOP_K), 1)
    vals, hots = [], []
    idx_out = jnp.zeros((TM, TOP_K), jnp.int32)
    for k in range(TOP_K):
        mx = jnp.max(x, axis=-1, keepdims=True)
        idx = jnp.min(jnp.where(x == mx, lane, N_EXP), axis=-1, keepdims=True)
        hot = lane == idx
        vals.append(mx)
        hots.append(hot)
        idx_out = jnp.where(col == k, idx, idx_out)
        x = jnp.where(hot, -jnp.inf, x)
    idx_ref[...] = idx_out

    es = [jnp.exp(v - vals[0]) for v in vals]
    denom = es[0] + es[1] + es[2] + es[3]
    comb = jnp.zeros((TM, TOP_K), F32)
    for k in range(TOP_K):
        comb = jnp.where(col == k, es[k] / denom, comb)
    comb_ref[...] = comb

    hot_sum = jnp.zeros((TM, N_EXP), F32)
    for k in range(TOP_K):
        hot_sum = hot_sum + jnp.where(hots[k], 1.0, 0.0)
    r_i = lax.broadcasted_iota(jnp.int32, (TM, TM), 0)
    c_i = lax.broadcasted_iota(jnp.int32, (TM, TM), 1)
    lower = jnp.where(c_i < r_i, 1.0, 0.0).astype(BF16)
    before = jnp.dot(lower, hot_sum.astype(BF16), preferred_element_type=F32) + cnt_sc[...]
    rank = jnp.zeros((TM, TOP_K), jnp.int32)
    for k in range(TOP_K):
        rk = jnp.sum(jnp.where(hots[k], before, 0.0), axis=-1, keepdims=True)
        rank = jnp.where(col == k, rk.astype(jnp.int32), rank)
    rank_ref[...] = rank
    cnt = cnt_sc[...] + jnp.sum(hot_sum, axis=0, keepdims=True)
    cnt_sc[...] = cnt
    cnt_ref[...] = cnt.astype(jnp.int32)


def _router(logits):
    T = logits.shape[0]
    tok = lambda w: pl.BlockSpec((TM, w), lambda i: (i, 0))
    return pl.pallas_call(
        _router_kernel,
        out_shape=(jax.ShapeDtypeStruct((T, TOP_K), jnp.int32), jax.ShapeDtypeStruct((T, TOP_K), F32),
                   jax.ShapeDtypeStruct((T, TOP_K), jnp.int32), jax.ShapeDtypeStruct((1, N_EXP), jnp.int32)),
        grid=(T // TM,),
        in_specs=[tok(N_EXP)],
        out_specs=(tok(TOP_K), tok(TOP_K), tok(TOP_K), pl.BlockSpec((1, N_EXP), lambda i: (0, 0))),
        scratch_shapes=[pltpu.VMEM((1, N_EXP), F32)],
        compiler_params=_cparams(("arbitrary",)),
    )(logits)


def _slot_tables(top_idx, rank, counts):
    t = top_idx.shape[0]
    n_assign = t * TOP_K
    counts = counts.reshape(N_EXP)
    padded = (counts + EXPERT_BLOCK - 1) // EXPERT_BLOCK * EXPERT_BLOCK
    pad_end = jnp.cumsum(padded)
    pad_start = pad_end - padded
    hot = top_idx[:, :, None] == jnp.arange(N_EXP, dtype=jnp.int32)[None, None, :]
    slot_of = jnp.sum(jnp.where(hot, pad_start[None, None, :], 0), axis=-1) + rank
    n_blocks = -(-n_assign // EXPERT_BLOCK) + N_EXP
    n_slots = n_blocks * EXPERT_BLOCK
    tok = jnp.broadcast_to(jnp.arange(t, dtype=jnp.int32)[:, None], (t, TOP_K))
    slot_tok = jnp.zeros((n_slots,), jnp.int32).at[slot_of.reshape(-1)].set(tok.reshape(-1))
    blk_start = jnp.arange(n_blocks, dtype=jnp.int32) * EXPERT_BLOCK
    blk_e = jnp.minimum(jnp.sum((pad_end[None, :] <= blk_start[:, None]).astype(jnp.int32), axis=1),
                        N_EXP - 1).astype(jnp.int32)
    n_used = (pad_end[-1] // EXPERT_BLOCK).astype(jnp.int32).reshape(1)
    return slot_tok, slot_of.T.reshape(-1).astype(jnp.int32), blk_e, n_used


def _rope_tables(tiles_per_b):
    n_lat = (tiles_per_b - 1) * TM
    pos = jnp.arange(n_lat, dtype=F32)
    t_row = jnp.floor(pos / GRID_W)
    t_col = pos - t_row * GRID_W
    n_freq = HD // 4
    inv_freq = ROPE_THETA ** (-jnp.arange(n_freq, dtype=F32) / n_freq)
    lane = jnp.arange(LANES)
    d = lane % HD
    use_col = (d // 32) == 1
    f = d % n_freq
    first = (d % 32) < n_freq
    ang = jnp.where(use_col[None, :], t_col[:, None], t_row[:, None]) * inv_freq[f][None, :]
    cos = jnp.cos(ang)
    sin = jnp.where(first[None, :], -jnp.sin(ang), jnp.sin(ang))
    cos = jnp.concatenate([jnp.ones((TM, LANES), F32), cos], axis=0)
    sin = jnp.concatenate([jnp.zeros((TM, LANES), F32), sin], axis=0)
    return cos, sin


def _permute_w_in(w):
    gk, gv, dk, dv = w[:, 0:128], w[:, 128:256], w[:, 256:768], w[:, 768:1280]
    return jnp.concatenate([gk, dk, w[:, 1280:2304], gv, dv, w[:, 2304:]], axis=1).astype(BF16)


def kernel(x, c, ctx, c_ctx, w_mod, b_mod, norm1_g, norm2_g, w_in, gqa_q_norm, gqa_k_norm, diff_q_norm, diff_k_norm, lam_q1, lam_k1, lam_q2, lam_k2, diff_subln_g, conv_dw_w, conv_dw_b, conv_ln_g, conv_ln_b, conv_pw_w, w_gqa_o, w_diff_o, b_gate, w_out, router_w, router_b, exp_w1, exp_b1, exp_w2, exp_b2):
    n_b, n_lat, d = x.shape
    n_ctx = ctx.shape[1]
    L = w_mod.shape[0]
    assert d == D and n_ctx == TM and n_lat % TM == 0 and n_b < 8
    tiles_per_b = (n_ctx + n_lat) // TM
    T = n_b * tiles_per_b * TM

    n_streams = N_STREAMS if n_b % N_STREAMS == 0 else 1
    nb_s = n_b // n_streams
    T_s = T // n_streams
    xs_all = jnp.concatenate([ctx, x], axis=1).reshape(n_streams, T_s, D)
    streams = [xs_all[s] for s in range(n_streams)]
    cc = jnp.zeros((n_streams, 8, D), F32)
    for s in range(n_streams):
        cc = cc.at[s, :nb_s].set(c[s * nb_s:(s + 1) * nb_s]).at[s, nb_s].set(c_ctx)
    mod = _modulation(cc.reshape(n_streams * 8, D), w_mod, b_mod).reshape(L, n_streams, 8, 6, D)
    cos_t, sin_t = _rope_tables(tiles_per_b)
    q_scale = HD ** -0.5 * math.log2(math.e)

    for l in range(L):
        lam_init = 0.8 - 0.6 * math.exp(-0.3 * l)
        tile2 = lambda g: jnp.tile(g, 2)
        qk_gain = jnp.concatenate(
            [tile2(gqa_k_norm[l])] * (GQA_KVH // 2) + [tile2(diff_k_norm[l])] * DIFF_H
            + [tile2(gqa_q_norm[l]) * q_scale] * (GQA_QH // 2)
            + [tile2(diff_q_norm[l]) * q_scale] * DIFF_H).reshape(1, QK_COLS)
        w_in_l = _permute_w_in(w_in[l])
        dw_w = jnp.concatenate([conv_dw_w[l], jnp.zeros((1, CONV_CH), F32)], axis=0)
        w_r_hi = router_w[l].astype(BF16)
        w_r = jnp.stack([w_r_hi, (router_w[l] - w_r_hi.astype(F32)).astype(BF16)])
        mix_w = (dw_w, conv_dw_b[l].reshape(1, CONV_CH),
                 conv_ln_g[l].reshape(1, CONV_CH), conv_ln_b[l].reshape(1, CONV_CH),
                 conv_pw_w[l].astype(BF16), w_gqa_o[l].astype(BF16), w_diff_o[l].astype(BF16),
                 w_out[l].astype(BF16), norm2_g[l].reshape(1, D), w_r, router_b[l].reshape(1, N_EXP))
        lam_refs = [v[l].reshape(1, HD) for v in (lam_q1, lam_k1, lam_q2, lam_k2)]
        for s in range(n_streams):
            xs = streams[s]
            mod_l = mod[l, s]
            kg, kd, qt, vtg, vtd, u, gates = _inproj(
                xs, mod_l, norm1_g[l].reshape(1, D), w_in_l, cos_t, sin_t, qk_gain,
                b_gate[l].reshape(1, GATE_COLS), tiles_per_b, nb_s)
            og = _attention(qt, kg, vtg, nb_s, tiles_per_b, diff=False)
            od = _attention(qt, kd, vtd, nb_s, tiles_per_b, diff=True, lam_refs=lam_refs,
                            subln_g=diff_subln_g[l].reshape(1, DIFF_V), lam_init=lam_init)
            xs, h2, logits = _mix(u, og, od, gates, xs, mod_l, *mix_w, tiles_per_b, nb_s)
            top_idx, comb, rank, counts = _router(logits)
            slot_tok, slot_of, blk_e, n_used = _slot_tables(top_idx, rank, counts)
            xb = h2[slot_tok]
            yb = _experts(xb, blk_e, n_used, exp_w1, exp_b1, exp_w2, exp_b2, l)
            yg = yb[slot_of].reshape(TOP_K, T_s, D)
            streams[s] = _combine(xs, yg, comb, mod_l, tiles_per_b, nb_s)

    out = jnp.stack(streams).reshape(n_b, n_ctx + n_lat, D)
    return out[:, n_ctx:, :]
```

```python
import functools
import math

import jax
import jax.numpy as jnp
from jax import lax
from jax.experimental import pallas as pl
from jax.experimental.pallas import tpu as pltpu

F32 = jnp.float32
BF16 = jnp.bfloat16

D = 1024
HD = 64
GRID_W = 64
ROPE_THETA = 10000.0
EPS = 1e-6
CONV_CH = 512
CONV_W = 31
CONV_HALO = 16
GQA_QH = 8
GQA_KVH = 2
DIFF_H = 4
DIFF_V = 128
N_EXP = 32
TOP_K = 4
FF = 1024
SWIGLU_ALPHA = 1.702
SWIGLU_LIMIT = 7.0
EXPERT_BLOCK = 256
FF_CHUNK = 1024
TM = 256
LANES = 128
SUBLANES = 8
SUM_ROWS = 16
N_STREAMS = 1
GQA_LAG = 6
DIFF_LAG = 5
UNROLL = 16

N_KHEADS = GQA_KVH + 2 * DIFF_H
N_QHEADS = GQA_QH + 2 * DIFF_H
QK_COLS = (N_KHEADS + N_QHEADS) * HD
V_COLS = GQA_KVH * HD + DIFF_H * DIFF_V
GATE_COLS = 3 * D
D_IN = QK_COLS + V_COLS + 2 * CONV_CH + GATE_COLS
VMEM_LIMIT = 56 * 1024 * 1024


def _cparams(sem):
    return pltpu.CompilerParams(dimension_semantics=sem, vmem_limit_bytes=VMEM_LIMIT)


def _mod_kernel(c_ref, w_ref, b_ref, o_ref):
    c = c_ref[...]
    s = c * (1.0 / (1.0 + jnp.exp(-c)))
    o_ref[0] = jnp.dot(s, w_ref[0], preferred_element_type=F32,
                       precision=lax.Precision.HIGHEST) + b_ref[0]


def _modulation(cc, w_mod, b_mod):
    L = w_mod.shape[0]
    R = cc.shape[0]
    tn = 1536
    return pl.pallas_call(
        _mod_kernel,
        out_shape=jax.ShapeDtypeStruct((L, R, 6 * D), F32),
        grid=(L, 6 * D // tn),
        in_specs=[pl.BlockSpec((R, D), lambda l, j: (0, 0)),
                  pl.BlockSpec((1, D, tn), lambda l, j: (l, 0, j)),
                  pl.BlockSpec((1, 1, tn), lambda l, j: (l, 0, j))],
        out_specs=pl.BlockSpec((1, R, tn), lambda l, j: (l, 0, j)),
        compiler_params=_cparams(("arbitrary", "arbitrary")),
    )(cc, w_mod, b_mod.reshape(L, 1, 6 * D))


def _modnorm(x, g, shift, scale):
    ms = jnp.mean(x * x, axis=-1, keepdims=True)
    return (x * lax.rsqrt(ms + EPS) * g) * (1.0 + scale) + shift


def _inproj_kernel(x_ref, mod_ref, g1_ref, w_ref, cos_ref, sin_ref, qkg_ref, bg_ref,
                   kg_out, kd_out, qt_out, vtg_out, vtd_out, u_out, g_out):
    h = _modnorm(x_ref[...], g1_ref[...], mod_ref[0, 0:1, :], mod_ref[0, 1:2, :]).astype(BF16)
    cos = cos_ref[...]
    sin = sin_ref[...]
    lane = lax.broadcasted_iota(jnp.int32, (1, LANES), 1)
    first = (lane % 32) < 16
    r_i = lax.broadcasted_iota(jnp.int32, (LANES, LANES), 0) // HD
    c_i = lax.broadcasted_iota(jnp.int32, (LANES, LANES), 1) // HD
    head_ones = jnp.where(r_i == c_i, 1.0, 0.0).astype(BF16)

    def proj(c0, c1):
        return jnp.dot(h, w_ref[:, c0:c1], preferred_element_type=F32)

    n_pairs = QK_COLS // LANES
    pairs_per_dot = 4
    for p0 in range(0, n_pairs, pairs_per_dot):
        p1 = min(p0 + pairs_per_dot, n_pairs)
        z = proj(p0 * LANES, p1 * LANES)
        for p in range(p0, p1):
            zc = z[:, (p - p0) * LANES:(p - p0 + 1) * LANES]
            ss = jnp.dot((zc * zc).astype(BF16), head_ones, preferred_element_type=F32)
            n = zc * lax.rsqrt(ss * (1.0 / HD) + EPS) * qkg_ref[:, p * LANES:(p + 1) * LANES]
            partner = jnp.where(first, pltpu.roll(n, LANES - 16, 1), pltpu.roll(n, 16, 1))
            r = n * cos + partner * sin
            if 2 * p < GQA_KVH:
                kg_out[2 * p] = r[:, :HD].astype(BF16)
                kg_out[2 * p + 1] = r[:, HD:].astype(BF16)
            elif 2 * p < N_KHEADS:
                kd_out[2 * p - GQA_KVH] = r[:, :HD].astype(BF16)
                kd_out[2 * p - GQA_KVH + 1] = r[:, HD:].astype(BF16)
            else:
                rt = r.T
                qh = 2 * p - N_KHEADS
                qt_out[qh] = rt[:HD, :].astype(BF16)
                qt_out[qh + 1] = rt[HD:, :].astype(BF16)

    z = proj(QK_COLS, QK_COLS + V_COLS)
    gv_pairs = GQA_KVH * HD // LANES
    for p in range(V_COLS // LANES):
        zt = z[:, p * LANES:(p + 1) * LANES].T.astype(BF16)
        if p < gv_pairs:
            vtg_out[0, p * LANES:(p + 1) * LANES, :] = zt
        else:
            vtd_out[0, (p - gv_pairs) * LANES:(p - gv_pairs + 1) * LANES, :] = zt

    c0 = QK_COLS + V_COLS
    a = proj(c0, c0 + 2 * CONV_CH)
    gl = a[:, CONV_CH:]
    u_out[...] = a[:, :CONV_CH] * (1.0 / (1.0 + jnp.exp(-gl)))

    c0 = c0 + 2 * CONV_CH
    for j in range(3):
        gp = proj(c0 + j * D, c0 + (j + 1) * D) + bg_ref[:, j * D:(j + 1) * D]
        g_out[:, j * D:(j + 1) * D] = (1.0 / (1.0 + jnp.exp(-gp))).astype(BF16)


def _mod_row(i, tiles_per_b, n_b):
    return jnp.where(i % tiles_per_b == 0, n_b, i // tiles_per_b)


def _inproj(x, mod_l, g1, w_perm, cos_t, sin_t, qk_gain, b_gate, tiles_per_b, n_b):
    T = x.shape[0]
    nt = T // TM
    out_shape = (
        jax.ShapeDtypeStruct((GQA_KVH, T, HD), BF16),
        jax.ShapeDtypeStruct((2 * DIFF_H, T, HD), BF16),
        jax.ShapeDtypeStruct((N_QHEADS, HD, T), BF16),
        jax.ShapeDtypeStruct((nt, GQA_KVH * HD, TM), BF16),
        jax.ShapeDtypeStruct((nt, DIFF_H * DIFF_V, TM), BF16),
        jax.ShapeDtypeStruct((T, CONV_CH), F32),
        jax.ShapeDtypeStruct((T, GATE_COLS), BF16),
    )
    in_specs = [
        pl.BlockSpec((TM, D), lambda i: (i, 0)),
        pl.BlockSpec((1, 6, D), lambda i: (_mod_row(i, tiles_per_b, n_b), 0, 0)),
        pl.BlockSpec((1, D), lambda i: (0, 0)),
        pl.BlockSpec((D, D_IN), lambda i: (0, 0)),
        pl.BlockSpec((TM, LANES), lambda i: (i % tiles_per_b, 0)),
        pl.BlockSpec((TM, LANES), lambda i: (i % tiles_per_b, 0)),
        pl.BlockSpec((1, QK_COLS), lambda i: (0, 0)),
        pl.BlockSpec((1, GATE_COLS), lambda i: (0, 0)),
    ]
    out_specs = (
        pl.BlockSpec((GQA_KVH, TM, HD), lambda i: (0, i, 0)),
        pl.BlockSpec((2 * DIFF_H, TM, HD), lambda i: (0, i, 0)),
        pl.BlockSpec((N_QHEADS, HD, TM), lambda i: (0, 0, i)),
        pl.BlockSpec((1, GQA_KVH * HD, TM), lambda i: (i, 0, 0)),
        pl.BlockSpec((1, DIFF_H * DIFF_V, TM), lambda i: (i, 0, 0)),
        pl.BlockSpec((TM, CONV_CH), lambda i: (i, 0)),
        pl.BlockSpec((TM, GATE_COLS), lambda i: (i, 0)),
    )
    return pl.pallas_call(
        _inproj_kernel, out_shape=out_shape, grid=(nt,),
        in_specs=in_specs, out_specs=out_specs,
        compiler_params=_cparams(("arbitrary",)),
    )(x, mod_l, g1, w_perm, cos_t, sin_t, qk_gain, b_gate)


def _attn_kernel(*refs, n_maps, n_v, share_k, dv, diff, lam_init):
    maps_per_v = n_maps // n_v
    if diff:
        (qt_ref, k_ref, vt_ref, lq1, lk1, lq2, lk2, sg_ref, o_ref, s0_sc, s1_sc, m_sc, acc_sc) = refs
    else:
        (qt_ref, k_ref, vt_ref, o_ref, s0_sc, s1_sc, m_sc, acc_sc) = refs
    qi = pl.program_id(2)
    n_iter = jnp.where(qi == 0, 0, (vt_ref.shape[0] - 1) // UNROLL)

    m_sc[...] = jnp.full(m_sc.shape, -jnp.inf, F32)
    acc_sc[...] = jnp.zeros(acc_sc.shape, F32)
    ones = jnp.ones((SUM_ROWS, TM), BF16)

    def score(j, s_sc, m):
        off = pl.multiple_of(j * TM, TM)
        kc = k_ref[0 if share_k else m, pl.ds(off, TM), :]
        s_sc[m] = jnp.dot(kc, qt_ref[m], preferred_element_type=F32)

    def softmax_pv(j, s_sc, m):
        h = m // maps_per_v
        vt = jnp.concatenate([vt_ref[j, h * dv:(h + 1) * dv, :], ones], axis=0)
        s = s_sc[m]
        m_prev = m_sc[m]
        m_new = jnp.maximum(m_prev, jnp.max(s, axis=0, keepdims=True))
        p = jnp.exp2(s - m_new).astype(BF16)
        alpha = jnp.exp2(m_prev - m_new)
        acc_sc[m] = alpha * acc_sc[m] + jnp.dot(vt, p, preferred_element_type=F32)
        m_sc[m] = m_new

    LAG = DIFF_LAG if diff else GQA_LAG
    bufs = (s0_sc, s1_sc)
    last_chunk = vt_ref.shape[0] - 1

    def score_pos(base_chunk, pp):
        c, m = divmod(pp, n_maps)
        score(jnp.minimum(base_chunk + c, last_chunk), bufs[c % 2], m)

    def pv_pos(base_chunk, pp):
        c, m = divmod(pp, n_maps)
        softmax_pv(base_chunk + c, bufs[c % 2], m)

    for pp in range(LAG):
        score_pos(0, pp)

    def body(i, carry):
        j = UNROLL * i
        for pp in range(UNROLL * n_maps):
            score_pos(j, pp + LAG)
            pv_pos(j, pp)
        return carry

    lax.fori_loop(0, n_iter, body, 0)
    for pp in range(n_maps):
        if pp + LAG < n_maps:
            score_pos(UNROLL * n_iter, pp + LAG)
        pv_pos(UNROLL * n_iter, pp)

    def out(m):
        return acc_sc[m, 0:dv, :] / acc_sc[m, dv:dv + 1, :]

    if diff:
        lam = (jnp.exp(jnp.sum(lq1[...] * lk1[...], axis=-1, keepdims=True))
               - jnp.exp(jnp.sum(lq2[...] * lk2[...], axis=-1, keepdims=True)) + lam_init)
        for h in range(n_v):
            o = (out(2 * h) - lam * out(2 * h + 1)).T
            ms = jnp.mean(o * o, axis=-1, keepdims=True)
            o_ref[:, h * dv:(h + 1) * dv] = (o * lax.rsqrt(ms + EPS) * sg_ref[...]
                                             * (1.0 - lam_init)).astype(o_ref.dtype)
    else:
        o = jnp.concatenate([out(m) for m in range(n_maps)], axis=0)
        o_ref[...] = o.T.astype(o_ref.dtype)


def _attention(qt, k3, vt3, n_b, tiles_per_b, *, diff, lam_refs=None, subln_g=None, lam_init=0.0):
    T = qt.shape[2]
    n_maps = 4
    if diff:
        n_groups, n_v, share_k, dv = DIFF_H // 2, 2, False, DIFF_V
        q_unit0 = GQA_QH // n_maps
        k_blk = n_maps
    else:
        n_groups, n_v, share_k, dv = GQA_KVH, 1, True, HD
        q_unit0 = 0
        k_blk = 1
    vt4 = vt3.reshape(n_b, tiles_per_b, vt3.shape[1], TM)
    tokens_per_b = tiles_per_b * TM
    in_specs = [
        pl.BlockSpec((n_maps, HD, TM), lambda b, g, q: (q_unit0 + g, 0, b * tiles_per_b + q)),
        pl.BlockSpec((k_blk, tokens_per_b, HD), lambda b, g, q: (g, b, 0)),
        pl.BlockSpec((None, tiles_per_b, n_v * dv, TM), lambda b, g, q: (b, 0, g, 0)),
    ]
    args = [qt, k3, vt4]
    if diff:
        in_specs += [pl.BlockSpec((1, HD), lambda b, g, q: (0, 0))] * 4
        in_specs += [pl.BlockSpec((1, DIFF_V), lambda b, g, q: (0, 0))]
        args += list(lam_refs) + [subln_g]
    out_cols = n_v * dv if diff else n_maps * dv
    kern = functools.partial(_attn_kernel, n_maps=n_maps, n_v=n_v, share_k=share_k, dv=dv, diff=diff,
                             lam_init=lam_init)
    return pl.pallas_call(
        kern,
        out_shape=jax.ShapeDtypeStruct((T, n_groups * out_cols), BF16),
        grid=(n_b, n_groups, tiles_per_b),
        in_specs=in_specs,
        out_specs=pl.BlockSpec((TM, out_cols), lambda b, g, q: (b * tiles_per_b + q, g)),
        scratch_shapes=[pltpu.VMEM((n_maps, TM, TM), F32), pltpu.VMEM((n_maps, TM, TM), F32),
                        pltpu.VMEM((n_maps, 1, TM), F32),
                        pltpu.VMEM((n_maps, dv + SUM_ROWS, TM), F32)],
        compiler_params=_cparams(("arbitrary", "arbitrary", "arbitrary")),
    )(*args)


def _mix_kernel(up_ref, uc_ref, un_ref, og_ref, od_ref, gt_ref, x_ref, mod_ref,
                dww_ref, dwb_ref, lng_ref, lnb_ref, pw_ref, wgo_ref, wdo_ref, wout_ref,
                g2_ref, wr_ref, br_ref,
                xo_ref, h2_ref, lg_ref, ext_sc, sh_sc, *, tiles_per_b):
    i = pl.program_id(0)
    ti = i % tiles_per_b
    has_prev = ti > 1
    has_next = jnp.logical_and(ti > 0, ti < tiles_per_b - 1)
    ext_sc[0:CONV_HALO, :] = jnp.where(has_prev, up_ref[...], 0.0)
    ext_sc[CONV_HALO:CONV_HALO + TM, :] = uc_ref[...]
    ext_sc[CONV_HALO + TM:, :] = jnp.where(has_next, un_ref[...], 0.0)
    acc = jnp.zeros((TM, CONV_CH), F32) + dwb_ref[...]
    base = CONV_HALO - CONV_W // 2
    span = TM + (CONV_W // SUBLANES) * SUBLANES
    for r in range(SUBLANES):
        taps = [w for w in range(CONV_W) if (base + w) % SUBLANES == r]
        if not taps:
            continue
        if r == 0:
            src = ext_sc
        else:
            sh_sc[r - 1] = ext_sc[r:r + span, :]
            src = sh_sc.at[r - 1]
        for w in taps:
            a = (base + w) // SUBLANES * SUBLANES
            acc = acc + src[a:a + TM, :] * dww_ref[w:w + 1, :]
    mu = jnp.mean(acc, axis=-1, keepdims=True)
    xc = acc - mu
    var = jnp.mean(xc * xc, axis=-1, keepdims=True)
    ln = xc * lax.rsqrt(var + EPS) * lng_ref[...] + lnb_ref[...]
    act = ln * (1.0 / (1.0 + jnp.exp(-ln)))
    br_a = jnp.dot(act.astype(BF16), pw_ref[...], preferred_element_type=F32)
    br_b = jnp.dot(og_ref[...], wgo_ref[...], preferred_element_type=F32)
    br_c = jnp.dot(od_ref[...], wdo_ref[...], preferred_element_type=F32)
    merged = (gt_ref[:, 0:D].astype(F32) * br_a + gt_ref[:, D:2 * D].astype(F32) * br_b
              + gt_ref[:, 2 * D:3 * D].astype(F32) * br_c)
    y = jnp.dot(merged.astype(BF16), wout_ref[...], preferred_element_type=F32)
    x_new = x_ref[...] + mod_ref[0, 2:3, :] * y
    xo_ref[...] = x_new
    h2 = _modnorm(x_new, g2_ref[...], mod_ref[0, 3:4, :], mod_ref[0, 4:5, :])
    h2_hi = h2.astype(BF16)
    h2_ref[...] = h2_hi
    h2_lo = (h2 - h2_hi.astype(F32)).astype(BF16)
    nt = (((1,), (1,)), ((), ()))
    lg_ref[...] = (lax.dot_general(wr_ref[0], h2_hi, nt, preferred_element_type=F32)
                   + lax.dot_general(wr_ref[0], h2_lo, nt, preferred_element_type=F32)
                   + lax.dot_general(wr_ref[1], h2_hi, nt, preferred_element_type=F32)) + br_ref[...]


def _mix(u, og, od, gates, x, mod_l, dw_w, dw_b, ln_g, ln_b, pw_w, w_gqa_o, w_diff_o, w_out,
         g2, w_r, b_r, tiles_per_b, n_b):
    T = x.shape[0]
    nt = T // TM
    halo_per_tile = TM // CONV_HALO
    n_halo = T // CONV_HALO
    full = lambda shape: pl.BlockSpec(shape, lambda i: (0,) * len(shape))
    in_specs = [
        pl.BlockSpec((CONV_HALO, CONV_CH), lambda i: (jnp.maximum(i * halo_per_tile - 1, 0), 0)),
        pl.BlockSpec((TM, CONV_CH), lambda i: (i, 0)),
        pl.BlockSpec((CONV_HALO, CONV_CH),
                     lambda i: (jnp.minimum((i + 1) * halo_per_tile, n_halo - 1), 0)),
        pl.BlockSpec((TM, GQA_QH * HD), lambda i: (i, 0)),
        pl.BlockSpec((TM, DIFF_H * DIFF_V), lambda i: (i, 0)),
        pl.BlockSpec((TM, GATE_COLS), lambda i: (i, 0)),
        pl.BlockSpec((TM, D), lambda i: (i, 0)),
        pl.BlockSpec((1, 6, D), lambda i: (_mod_row(i, tiles_per_b, n_b), 0, 0)),
        full((CONV_W + 1, CONV_CH)), full((1, CONV_CH)), full((1, CONV_CH)), full((1, CONV_CH)),
        full((CONV_CH, D)), full((GQA_QH * HD, D)), full((DIFF_H * DIFF_V, D)), full((D, D)),
        full((1, D)), full((2, N_EXP, D)), full((N_EXP, 1)),
    ]
    out_shape = (jax.ShapeDtypeStruct((T, D), F32), jax.ShapeDtypeStruct((T, D), BF16),
                 jax.ShapeDtypeStruct((N_EXP, T), F32))
    out_specs = (pl.BlockSpec((TM, D), lambda i: (i, 0)), pl.BlockSpec((TM, D), lambda i: (i, 0)),
                 pl.BlockSpec((N_EXP, TM), lambda i: (0, i)))
    return pl.pallas_call(
        functools.partial(_mix_kernel, tiles_per_b=tiles_per_b),
        out_shape=out_shape, grid=(nt,), in_specs=in_specs, out_specs=out_specs,
        scratch_shapes=[pltpu.VMEM((TM + 2 * CONV_HALO, CONV_CH), F32),
                        pltpu.VMEM((SUBLANES - 1, TM + (CONV_W // SUBLANES) * SUBLANES, CONV_CH), F32)],
        compiler_params=_cparams(("arbitrary",)),
    )(u, u, u, og, od, gates, x, mod_l, dw_w, dw_b, ln_g, ln_b, pw_w, w_gqa_o, w_diff_o, w_out,
      g2, w_r, b_r)


def _expert_kernel(be_ref, nb_ref, x_ref, w1_ref, b1_ref, w2_ref, b2_ref, y_ref, w1_sc, w2_sc):
    i = pl.program_id(0)
    prev_e = be_ref[jnp.maximum(i - 1, 0)]
    new_expert = jnp.logical_or(i == 0, be_ref[i] != prev_e)

    @pl.when(new_expert)
    def _():
        w1_sc[...] = w1_ref[0].astype(BF16)
        w2_sc[...] = w2_ref[0].astype(BF16)

    @pl.when(i < nb_ref[0])
    def _():
        x = x_ref[...]
        y = jnp.zeros((EXPERT_BLOCK, D), F32) + b2_ref[0]
        for c0 in range(0, FF, FF_CHUNK):
            c1 = c0 + FF_CHUNK
            g = jnp.dot(x, w1_sc[:, c0:c1], preferred_element_type=F32) + b1_ref[0, :, c0:c1]
            li = (jnp.dot(x, w1_sc[:, FF + c0:FF + c1], preferred_element_type=F32)
                  + b1_ref[0, :, FF + c0:FF + c1])
            gate = jnp.minimum(g, SWIGLU_LIMIT)
            lin = jnp.clip(li, -SWIGLU_LIMIT, SWIGLU_LIMIT)
            act = (lin + 1.0) * (gate * (1.0 / (1.0 + jnp.exp(-gate * SWIGLU_ALPHA))))
            y = y + jnp.dot(act.astype(BF16), w2_sc[c0:c1, :], preferred_element_type=F32)
        y_ref[...] = y.astype(y_ref.dtype)

    @pl.when(i >= nb_ref[0])
    def _():
        y_ref[...] = jnp.zeros(y_ref.shape, y_ref.dtype)


def _experts(xb, blk_e, n_used, w1, b1, w2, b2, l):
    n_slots = xb.shape[0]
    n_blocks = n_slots // EXPERT_BLOCK
    L = w1.shape[0]
    grid_spec = pltpu.PrefetchScalarGridSpec(
        num_scalar_prefetch=2, grid=(n_blocks,),
        in_specs=[
            pl.BlockSpec((EXPERT_BLOCK, D), lambda i, be, nb: (i, 0)),
            pl.BlockSpec((None, 1, D, 2 * FF), lambda i, be, nb: (l, be[i], 0, 0)),
            pl.BlockSpec((None, 1, 1, 2 * FF), lambda i, be, nb: (l, be[i], 0, 0)),
            pl.BlockSpec((None, 1, FF, D), lambda i, be, nb: (l, be[i], 0, 0)),
            pl.BlockSpec((None, 1, 1, D), lambda i, be, nb: (l, be[i], 0, 0)),
        ],
        out_specs=pl.BlockSpec((EXPERT_BLOCK, D), lambda i, be, nb: (i, 0)),
        scratch_shapes=[pltpu.VMEM((D, 2 * FF), BF16), pltpu.VMEM((FF, D), BF16)],
    )
    return pl.pallas_call(
        _expert_kernel, grid_spec=grid_spec,
        out_shape=jax.ShapeDtypeStruct((n_slots, D), BF16),
        compiler_params=_cparams(("arbitrary",)),
    )(blk_e, n_used, xb, w1, b1.reshape(L, N_EXP, 1, 2 * FF), w2, b2.reshape(L, N_EXP, 1, D))


def _combine_kernel(x_ref, yg_ref, cw_ref, mod_ref, o_ref):
    cw = cw_ref[...]
    f = jnp.zeros((TM, D), F32)
    for k in range(TOP_K):
        f = f + cw[:, k:k + 1] * yg_ref[k].astype(F32)
    o_ref[...] = x_ref[...] + mod_ref[0, 5:6, :] * f


def _combine(x, yg, comb, mod_l, tiles_per_b, n_b):
    T = x.shape[0]
    return pl.pallas_call(
        _combine_kernel,
        out_shape=jax.ShapeDtypeStruct((T, D), F32),
        grid=(T // TM,),
        in_specs=[pl.BlockSpec((TM, D), lambda i: (i, 0)),
                  pl.BlockSpec((TOP_K, TM, D), lambda i: (0, i, 0)),
                  pl.BlockSpec((TM, TOP_K), lambda i: (i, 0)),
                  pl.BlockSpec((1, 6, D), lambda i: (_mod_row(i, tiles_per_b, n_b), 0, 0))],
        out_specs=pl.BlockSpec((TM, D), lambda i: (i, 0)),
        compiler_params=_cparams(("arbitrary",)),
    )(x, yg, comb, mod_l)


def _router_kernel(lg_ref, idx_ref, comb_ref, rank_ref, cnt_ref, cnt_sc):
    i = pl.program_id(0)

    @pl.when(i == 0)
    def _():
        cnt_sc[...] = jnp.zeros(cnt_sc.shape, F32)

    x = lg_ref[...]
    sub = lax.broadcasted_iota(jnp.int32, (N_EXP, TM), 0)
    row = lax.broadcasted_iota(jnp.int32, (TOP_K, TM), 0)
    vals, hots = [], []
    idx_out = jnp.zeros((TOP_K, TM), jnp.int32)
    for k in range(TOP_K):
        mx = jnp.max(x, axis=0, keepdims=True)
        idx = jnp.min(jnp.where(x == mx, sub, N_EXP), axis=0, keepdims=True)
        hot = sub == idx
        vals.append(mx)
        hots.append(hot)
        idx_out = jnp.where(row == k, idx, idx_out)
        x = jnp.where(hot, -jnp.inf, x)
    idx_ref[...] = idx_out

    es = [jnp.exp(v - vals[0]) for v in vals]
    denom = es[0] + es[1] + es[2] + es[3]
    comb = jnp.zeros((TOP_K, TM), F32)
    for k in range(TOP_K):
        comb = jnp.where(row == k, es[k] / denom, comb)
    comb_ref[...] = comb

    hot_sum = jnp.zeros((N_EXP, TM), F32)
    for k in range(TOP_K):
        hot_sum = hot_sum + jnp.where(hots[k], 1.0, 0.0)
    r_i = lax.broadcasted_iota(jnp.int32, (TM, TM), 0)
    c_i = lax.broadcasted_iota(jnp.int32, (TM, TM), 1)
    earlier = jnp.where(r_i < c_i, 1.0, 0.0).astype(BF16)
    before = jnp.dot(hot_sum.astype(BF16), earlier, preferred_element_type=F32) + cnt_sc[...]
    rank = jnp.zeros((TOP_K, TM), jnp.int32)
    for k in range(TOP_K):
        rk = jnp.sum(jnp.where(hots[k], before, 0.0), axis=0, keepdims=True)
        rank = jnp.where(row == k, rk.astype(jnp.int32), rank)
    rank_ref[...] = rank
    cnt = cnt_sc[...] + jnp.sum(hot_sum, axis=1, keepdims=True)
    cnt_sc[...] = cnt
    cnt_ref[...] = cnt.astype(jnp.int32)


def _router(logits_t):
    T = logits_t.shape[1]
    tok = lambda r: pl.BlockSpec((r, TM), lambda i: (0, i))
    return pl.pallas_call(
        _router_kernel,
        out_shape=(jax.ShapeDtypeStruct((TOP_K, T), jnp.int32), jax.ShapeDtypeStruct((TOP_K, T), F32),
                   jax.ShapeDtypeStruct((TOP_K, T), jnp.int32), jax.ShapeDtypeStruct((N_EXP, 1), jnp.int32)),
        grid=(T // TM,),
        in_specs=[tok(N_EXP)],
        out_specs=(tok(TOP_K), tok(TOP_K), tok(TOP_K), pl.BlockSpec((N_EXP, 1), lambda i: (0, 0))),
        scratch_shapes=[pltpu.VMEM((N_EXP, 1), F32)],
        compiler_params=_cparams(("arbitrary",)),
    )(logits_t)


def _slot_tables(top_idx, rank, counts):
    t = top_idx.shape[1]
    n_assign = t * TOP_K
    counts = counts.reshape(N_EXP)
    padded = (counts + EXPERT_BLOCK - 1) // EXPERT_BLOCK * EXPERT_BLOCK
    pad_end = jnp.cumsum(padded)
    pad_start = pad_end - padded
    start = jnp.cumsum(counts) - counts
    hot = top_idx[:, :, None] == jnp.arange(N_EXP, dtype=jnp.int32)[None, None, :]
    slot_of = (jnp.sum(jnp.where(hot, pad_start[None, None, :], 0), axis=-1) + rank).reshape(-1)
    n_blocks = -(-n_assign // EXPERT_BLOCK) + N_EXP
    blk_start = jnp.arange(n_blocks, dtype=jnp.int32) * EXPERT_BLOCK
    blk_e = jnp.minimum(jnp.sum((pad_end[None, :] <= blk_start[:, None]).astype(jnp.int32), axis=1),
                        N_EXP - 1).astype(jnp.int32)
    n_used = (pad_end[-1] // EXPERT_BLOCK).astype(jnp.int32).reshape(1)
    tok = jnp.broadcast_to(jnp.arange(t, dtype=jnp.int32)[None, :], (TOP_K, t)).reshape(-1)
    _, tok_sorted = lax.sort_key_val(slot_of, tok)
    hot_b = blk_e[:, None] == jnp.arange(N_EXP, dtype=jnp.int32)[None, :]
    sel = lambda v: jnp.sum(jnp.where(hot_b, v[None, :], 0), axis=1)
    r0 = blk_start - sel(pad_start)
    r = r0[:, None] + jnp.arange(EXPERT_BLOCK, dtype=jnp.int32)[None, :]
    valid = r < sel(counts)[:, None]
    src = jnp.where(valid, sel(start)[:, None] + r, 0)
    slot_tok = jnp.where(valid, tok_sorted[src.reshape(-1)].reshape(src.shape), 0).reshape(-1)
    return slot_tok, slot_of.astype(jnp.int32), blk_e, n_used


def _rope_tables(tiles_per_b):
    n_lat = (tiles_per_b - 1) * TM
    pos = jnp.arange(n_lat, dtype=F32)
    t_row = jnp.floor(pos / GRID_W)
    t_col = pos - t_row * GRID_W
    n_freq = HD // 4
    inv_freq = ROPE_THETA ** (-jnp.arange(n_freq, dtype=F32) / n_freq)
    lane = jnp.arange(LANES)
    d = lane % HD
    use_col = (d // 32) == 1
    f = d % n_freq
    first = (d % 32) < n_freq
    ang = jnp.where(use_col[None, :], t_col[:, None], t_row[:, None]) * inv_freq[f][None, :]
    cos = jnp.cos(ang)
    sin = jnp.where(first[None, :], -jnp.sin(ang), jnp.sin(ang))
    cos = jnp.concatenate([jnp.ones((TM, LANES), F32), cos], axis=0)
    sin = jnp.concatenate([jnp.zeros((TM, LANES), F32), sin], axis=0)
    return cos, sin


def _permute_w_in(w):
    gk, gv, dk, dv = w[:, 0:128], w[:, 128:256], w[:, 256:768], w[:, 768:1280]
    return jnp.concatenate([gk, dk, w[:, 1280:2304], gv, dv, w[:, 2304:]], axis=1).astype(BF16)


def kernel(x, c, ctx, c_ctx, w_mod, b_mod, norm1_g, norm2_g, w_in, gqa_q_norm, gqa_k_norm, diff_q_norm, diff_k_norm, lam_q1, lam_k1, lam_q2, lam_k2, diff_subln_g, conv_dw_w, conv_dw_b, conv_ln_g, conv_ln_b, conv_pw_w, w_gqa_o, w_diff_o, b_gate, w_out, router_w, router_b, exp_w1, exp_b1, exp_w2, exp_b2):
    n_b, n_lat, d = x.shape
    n_ctx = ctx.shape[1]
    L = w_mod.shape[0]
    assert d == D and n_ctx == TM and n_lat % TM == 0 and n_b < 8
    tiles_per_b = (n_ctx + n_lat) // TM
    assert (tiles_per_b - 1) % UNROLL == 0
    T = n_b * tiles_per_b * TM

    n_streams = N_STREAMS if n_b % N_STREAMS == 0 else 1
    nb_s = n_b // n_streams
    T_s = T // n_streams
    xs_all = jnp.concatenate([ctx, x], axis=1).reshape(n_streams, T_s, D)
    streams = [xs_all[s] for s in range(n_streams)]
    cc = jnp.zeros((n_streams, 8, D), F32)
    for s in range(n_streams):
        cc = cc.at[s, :nb_s].set(c[s * nb_s:(s + 1) * nb_s]).at[s, nb_s].set(c_ctx)
    mod = _modulation(cc.reshape(n_streams * 8, D), w_mod, b_mod).reshape(L, n_streams, 8, 6, D)
    cos_t, sin_t = _rope_tables(tiles_per_b)
    q_scale = HD ** -0.5 * math.log2(math.e)

    for l in range(L):
        lam_init = 0.8 - 0.6 * math.exp(-0.3 * l)
        tile2 = lambda g: jnp.tile(g, 2)
        qk_gain = jnp.concatenate(
            [tile2(gqa_k_norm[l])] * (GQA_KVH // 2) + [tile2(diff_k_norm[l])] * DIFF_H
            + [tile2(gqa_q_norm[l]) * q_scale] * (GQA_QH // 2)
            + [tile2(diff_q_norm[l]) * q_scale] * DIFF_H).reshape(1, QK_COLS)
        w_in_l = _permute_w_in(w_in[l])
        dw_w = jnp.concatenate([conv_dw_w[l], jnp.zeros((1, CONV_CH), F32)], axis=0)
        w_r_t = router_w[l].T
        w_r_hi = w_r_t.astype(BF16)
        w_r = jnp.stack([w_r_hi, (w_r_t - w_r_hi.astype(F32)).astype(BF16)])
        mix_w = (dw_w, conv_dw_b[l].reshape(1, CONV_CH),
                 conv_ln_g[l].reshape(1, CONV_CH), conv_ln_b[l].reshape(1, CONV_CH),
                 conv_pw_w[l].astype(BF16), w_gqa_o[l].astype(BF16), w_diff_o[l].astype(BF16),
                 w_out[l].astype(BF16), norm2_g[l].reshape(1, D), w_r, router_b[l].reshape(N_EXP, 1))
        lam_refs = [v[l].reshape(1, HD) for v in (lam_q1, lam_k1, lam_q2, lam_k2)]
        for s in range(n_streams):
            xs = streams[s]
            mod_l = mod[l, s]
            kg, kd, qt, vtg, vtd, u, gates = _inproj(
                xs, mod_l, norm1_g[l].reshape(1, D), w_in_l, cos_t, sin_t, qk_gain,
                b_gate[l].reshape(1, GATE_COLS), tiles_per_b, nb_s)
            og = _attention(qt, kg, vtg, nb_s, tiles_per_b, diff=False)
            od = _attention(qt, kd, vtd, nb_s, tiles_per_b, diff=True, lam_refs=lam_refs,
                            subln_g=diff_subln_g[l].reshape(1, DIFF_V), lam_init=lam_init)
            xs, h2, logits_t = _mix(u, og, od, gates, xs, mod_l, *mix_w, tiles_per_b, nb_s)
            top_idx, comb, rank, counts = _router(logits_t)
            slot_tok, slot_of, blk_e, n_used = _slot_tables(top_idx, rank, counts)
            xb = h2[slot_tok]
            yb = _experts(xb, blk_e, n_used, exp_w1, exp_b1, exp_w2, exp_b2, l)
            yg = yb[slot_of].reshape(TOP_K, T_s, D)
            streams[s] = _combine(xs, yg, comb.T, mod_l, tiles_per_b, nb_s)

    out = jnp.stack(streams).reshape(n_b, n_ctx + n_lat, D)
    return out[:, n_ctx:, :]
```

```python
import functools
import math

import jax
import jax.numpy as jnp
from jax import lax
from jax.experimental import pallas as pl
from jax.experimental.pallas import tpu as pltpu

F32 = jnp.float32
BF16 = jnp.bfloat16

D = 1024
HD = 64
GRID_W = 64
ROPE_THETA = 10000.0
EPS = 1e-6
CONV_CH = 512
CONV_W = 31
CONV_HALO = 16
GQA_QH = 8
GQA_KVH = 2
DIFF_H = 4
DIFF_V = 128
N_EXP = 32
TOP_K = 4
FF = 1024
SWIGLU_ALPHA = 1.702
SWIGLU_LIMIT = 7.0
EXPERT_BLOCK = 256
FF_CHUNK = 1024
TM = 256
LANES = 128
SUBLANES = 8
SUM_ROWS = 16
N_STREAMS = 1
GQA_LAG = 6
DIFF_LAG = 5
UNROLL = 16

N_KHEADS = GQA_KVH + 2 * DIFF_H
N_QHEADS = GQA_QH + 2 * DIFF_H
QK_COLS = (N_KHEADS + N_QHEADS) * HD
V_COLS = GQA_KVH * HD + DIFF_H * DIFF_V
GATE_COLS = 3 * D
D_IN = QK_COLS + V_COLS + 2 * CONV_CH + GATE_COLS
VMEM_LIMIT = 56 * 1024 * 1024


def _cparams(sem):
    return pltpu.CompilerParams(dimension_semantics=sem, vmem_limit_bytes=VMEM_LIMIT)


def _mod_kernel(c_ref, w_ref, b_ref, o_ref):
    c = c_ref[...]
    s = c * (1.0 / (1.0 + jnp.exp(-c)))
    o_ref[0] = jnp.dot(s, w_ref[0], preferred_element_type=F32,
                       precision=lax.Precision.HIGHEST) + b_ref[0]


def _modulation(cc, w_mod, b_mod):
    L = w_mod.shape[0]
    R = cc.shape[0]
    tn = 1536
    return pl.pallas_call(
        _mod_kernel,
        out_shape=jax.ShapeDtypeStruct((L, R, 6 * D), F32),
        grid=(L, 6 * D // tn),
        in_specs=[pl.BlockSpec((R, D), lambda l, j: (0, 0)),
                  pl.BlockSpec((1, D, tn), lambda l, j: (l, 0, j)),
                  pl.BlockSpec((1, 1, tn), lambda l, j: (l, 0, j))],
        out_specs=pl.BlockSpec((1, R, tn), lambda l, j: (l, 0, j)),
        compiler_params=_cparams(("arbitrary", "arbitrary")),
    )(cc, w_mod, b_mod.reshape(L, 1, 6 * D))


def _modnorm(x, g, shift, scale):
    ms = jnp.mean(x * x, axis=-1, keepdims=True)
    return (x * lax.rsqrt(ms + EPS) * g) * (1.0 + scale) + shift


def _inproj_kernel(x_ref, mod_ref, g1_ref, w_ref, cos_ref, sin_ref, qkg_ref, bg_ref,
                   kg_out, kd_out, qt_out, vtg_out, vtd_out, u_out, g_out):
    h = _modnorm(x_ref[...], g1_ref[...], mod_ref[0, 0:1, :], mod_ref[0, 1:2, :]).astype(BF16)
    cos = cos_ref[...]
    sin = sin_ref[...]
    lane = lax.broadcasted_iota(jnp.int32, (1, LANES), 1)
    first = (lane % 32) < 16
    r_i = lax.broadcasted_iota(jnp.int32, (LANES, LANES), 0) // HD
    c_i = lax.broadcasted_iota(jnp.int32, (LANES, LANES), 1) // HD
    head_ones = jnp.where(r_i == c_i, 1.0, 0.0).astype(BF16)

    def proj(c0, c1):
        return jnp.dot(h, w_ref[:, c0:c1], preferred_element_type=F32)

    n_pairs = QK_COLS // LANES
    pairs_per_dot = 4
    for p0 in range(0, n_pairs, pairs_per_dot):
        p1 = min(p0 + pairs_per_dot, n_pairs)
        z = proj(p0 * LANES, p1 * LANES)
        for p in range(p0, p1):
            zc = z[:, (p - p0) * LANES:(p - p0 + 1) * LANES]
            ss = jnp.dot((zc * zc).astype(BF16), head_ones, preferred_element_type=F32)
            n = zc * lax.rsqrt(ss * (1.0 / HD) + EPS) * qkg_ref[:, p * LANES:(p + 1) * LANES]
            partner = jnp.where(first, pltpu.roll(n, LANES - 16, 1), pltpu.roll(n, 16, 1))
            r = n * cos + partner * sin
            if 2 * p < GQA_KVH:
                kg_out[2 * p] = r[:, :HD].astype(BF16)
                kg_out[2 * p + 1] = r[:, HD:].astype(BF16)
            elif 2 * p < N_KHEADS:
                kd_out[2 * p - GQA_KVH] = r[:, :HD].astype(BF16)
                kd_out[2 * p - GQA_KVH + 1] = r[:, HD:].astype(BF16)
            else:
                rt = r.T
                qh = 2 * p - N_KHEADS
                qt_out[qh] = rt[:HD, :].astype(BF16)
                qt_out[qh + 1] = rt[HD:, :].astype(BF16)

    z = proj(QK_COLS, QK_COLS + V_COLS)
    gv_pairs = GQA_KVH * HD // LANES
    for p in range(V_COLS // LANES):
        zt = z[:, p * LANES:(p + 1) * LANES].T.astype(BF16)
        if p < gv_pairs:
            vtg_out[0, p * LANES:(p + 1) * LANES, :] = zt
        else:
            vtd_out[0, (p - gv_pairs) * LANES:(p - gv_pairs + 1) * LANES, :] = zt

    c0 = QK_COLS + V_COLS
    a = proj(c0, c0 + 2 * CONV_CH)
    gl = a[:, CONV_CH:]
    u_out[...] = a[:, :CONV_CH] * (1.0 / (1.0 + jnp.exp(-gl)))

    c0 = c0 + 2 * CONV_CH
    for j in range(3):
        gp = proj(c0 + j * D, c0 + (j + 1) * D) + bg_ref[:, j * D:(j + 1) * D]
        g_out[:, j * D:(j + 1) * D] = (1.0 / (1.0 + jnp.exp(-gp))).astype(BF16)


def _mod_row(i, tiles_per_b, n_b):
    return jnp.where(i % tiles_per_b == 0, n_b, i // tiles_per_b)


def _inproj(x, mod_l, g1, w_perm, cos_t, sin_t, qk_gain, b_gate, tiles_per_b, n_b):
    T = x.shape[0]
    nt = T // TM
    out_shape = (
        jax.ShapeDtypeStruct((GQA_KVH, T, HD), BF16),
        jax.ShapeDtypeStruct((2 * DIFF_H, T, HD), BF16),
        jax.ShapeDtypeStruct((N_QHEADS, HD, T), BF16),
        jax.ShapeDtypeStruct((nt, GQA_KVH * HD, TM), BF16),
        jax.ShapeDtypeStruct((nt, DIFF_H * DIFF_V, TM), BF16),
        jax.ShapeDtypeStruct((T, CONV_CH), F32),
        jax.ShapeDtypeStruct((T, GATE_COLS), BF16),
    )
    in_specs = [
        pl.BlockSpec((TM, D), lambda i: (i, 0)),
        pl.BlockSpec((1, 6, D), lambda i: (_mod_row(i, tiles_per_b, n_b), 0, 0)),
        pl.BlockSpec((1, D), lambda i: (0, 0)),
        pl.BlockSpec((D, D_IN), lambda i: (0, 0)),
        pl.BlockSpec((TM, LANES), lambda i: (i % tiles_per_b, 0)),
        pl.BlockSpec((TM, LANES), lambda i: (i % tiles_per_b, 0)),
        pl.BlockSpec((1, QK_COLS), lambda i: (0, 0)),
        pl.BlockSpec((1, GATE_COLS), lambda i: (0, 0)),
    ]
    out_specs = (
        pl.BlockSpec((GQA_KVH, TM, HD), lambda i: (0, i, 0)),
        pl.BlockSpec((2 * DIFF_H, TM, HD), lambda i: (0, i, 0)),
        pl.BlockSpec((N_QHEADS, HD, TM), lambda i: (0, 0, i)),
        pl.BlockSpec((1, GQA_KVH * HD, TM), lambda i: (i, 0, 0)),
        pl.BlockSpec((1, DIFF_H * DIFF_V, TM), lambda i: (i, 0, 0)),
        pl.BlockSpec((TM, CONV_CH), lambda i: (i, 0)),
        pl.BlockSpec((TM, GATE_COLS), lambda i: (i, 0)),
    )
    return pl.pallas_call(
        _inproj_kernel, out_shape=out_shape, grid=(nt,),
        in_specs=in_specs, out_specs=out_specs,
        compiler_params=_cparams(("arbitrary",)),
    )(x, mod_l, g1, w_perm, cos_t, sin_t, qk_gain, b_gate)


def _attn_kernel(*refs, n_maps, n_v, share_k, dv, diff, lam_init):
    maps_per_v = n_maps // n_v
    if diff:
        (qt_ref, k_ref, vt_ref, lq1, lk1, lq2, lk2, sg_ref, o_ref, s0_sc, s1_sc, m_sc, acc_sc) = refs
    else:
        (qt_ref, k_ref, vt_ref, o_ref, s0_sc, s1_sc, m_sc, acc_sc) = refs
    qi = pl.program_id(2)
    n_iter = jnp.where(qi == 0, 0, (vt_ref.shape[0] - 1) // UNROLL)

    m_sc[...] = jnp.full(m_sc.shape, -jnp.inf, F32)
    acc_sc[...] = jnp.zeros(acc_sc.shape, F32)
    ones = jnp.ones((SUM_ROWS, TM), BF16)

    def score(j, s_sc, m):
        off = pl.multiple_of(j * TM, TM)
        kc = k_ref[0 if share_k else m, pl.ds(off, TM), :]
        s_sc[m] = jnp.dot(kc, qt_ref[m], preferred_element_type=F32)

    def softmax_pv(j, s_sc, m):
        h = m // maps_per_v
        vt = jnp.concatenate([vt_ref[j, h * dv:(h + 1) * dv, :], ones], axis=0)
        s = s_sc[m]
        m_prev = m_sc[m]
        m_new = jnp.maximum(m_prev, jnp.max(s, axis=0, keepdims=True))
        p = jnp.exp2(s - m_new).astype(BF16)
        alpha = jnp.exp2(m_prev - m_new)
        acc_sc[m] = alpha * acc_sc[m] + jnp.dot(vt, p, preferred_element_type=F32)
        m_sc[m] = m_new

    LAG = DIFF_LAG if diff else GQA_LAG
    bufs = (s0_sc, s1_sc)
    last_chunk = vt_ref.shape[0] - 1

    def score_pos(base_chunk, pp):
        c, m = divmod(pp, n_maps)
        score(jnp.minimum(base_chunk + c, last_chunk), bufs[c % 2], m)

    def pv_pos(base_chunk, pp):
        c, m = divmod(pp, n_maps)
        softmax_pv(base_chunk + c, bufs[c % 2], m)

    for pp in range(LAG):
        score_pos(0, pp)

    def body(i, carry):
        j = UNROLL * i
        for pp in range(UNROLL * n_maps):
            score_pos(j, pp + LAG)
            pv_pos(j, pp)
        return carry

    lax.fori_loop(0, n_iter, body, 0)
    for pp in range(n_maps):
        if pp + LAG < n_maps:
            score_pos(UNROLL * n_iter, pp + LAG)
        pv_pos(UNROLL * n_iter, pp)

    def out(m):
        return acc_sc[m, 0:dv, :] / acc_sc[m, dv:dv + 1, :]

    if diff:
        lam = (jnp.exp(jnp.sum(lq1[...] * lk1[...], axis=-1, keepdims=True))
               - jnp.exp(jnp.sum(lq2[...] * lk2[...], axis=-1, keepdims=True)) + lam_init)
        for h in range(n_v):
            o = (out(2 * h) - lam * out(2 * h + 1)).T
            ms = jnp.mean(o * o, axis=-1, keepdims=True)
            o_ref[:, h * dv:(h + 1) * dv] = (o * lax.rsqrt(ms + EPS) * sg_ref[...]
                                             * (1.0 - lam_init)).astype(o_ref.dtype)
    else:
        o = jnp.concatenate([out(m) for m in range(n_maps)], axis=0)
        o_ref[...] = o.T.astype(o_ref.dtype)


def _attention(qt, k3, vt3, n_b, tiles_per_b, *, diff, lam_refs=None, subln_g=None, lam_init=0.0):
    T = qt.shape[2]
    n_maps = 4
    if diff:
        n_groups, n_v, share_k, dv = DIFF_H // 2, 2, False, DIFF_V
        q_unit0 = GQA_QH // n_maps
        k_blk = n_maps
    else:
        n_groups, n_v, share_k, dv = GQA_KVH, 1, True, HD
        q_unit0 = 0
        k_blk = 1
    vt4 = vt3.reshape(n_b, tiles_per_b, vt3.shape[1], TM)
    tokens_per_b = tiles_per_b * TM
    in_specs = [
        pl.BlockSpec((n_maps, HD, TM), lambda b, g, q: (q_unit0 + g, 0, b * tiles_per_b + q)),
        pl.BlockSpec((k_blk, tokens_per_b, HD), lambda b, g, q: (g, b, 0)),
        pl.BlockSpec((None, tiles_per_b, n_v * dv, TM), lambda b, g, q: (b, 0, g, 0)),
    ]
    args = [qt, k3, vt4]
    if diff:
        in_specs += [pl.BlockSpec((1, HD), lambda b, g, q: (0, 0))] * 4
        in_specs += [pl.BlockSpec((1, DIFF_V), lambda b, g, q: (0, 0))]
        args += list(lam_refs) + [subln_g]
    out_cols = n_v * dv if diff else n_maps * dv
    kern = functools.partial(_attn_kernel, n_maps=n_maps, n_v=n_v, share_k=share_k, dv=dv, diff=diff,
                             lam_init=lam_init)
    return pl.pallas_call(
        kern,
        out_shape=jax.ShapeDtypeStruct((T, n_groups * out_cols), BF16),
        grid=(n_b, n_groups, tiles_per_b),
        in_specs=in_specs,
        out_specs=pl.BlockSpec((TM, out_cols), lambda b, g, q: (b * tiles_per_b + q, g)),
        scratch_shapes=[pltpu.VMEM((n_maps, TM, TM), F32), pltpu.VMEM((n_maps, TM, TM), F32),
                        pltpu.VMEM((n_maps, 1, TM), F32),
                        pltpu.VMEM((n_maps, dv + SUM_ROWS, TM), F32)],
        compiler_params=_cparams(("arbitrary", "arbitrary", "arbitrary")),
    )(*args)


def _mix_kernel(up_ref, uc_ref, un_ref, og_ref, od_ref, gt_ref, x_ref, mod_ref,
                dww_ref, dwb_ref, lng_ref, lnb_ref, pw_ref, wgo_ref, wdo_ref, wout_ref,
                g2_ref, wr_ref, br_ref,
                xo_ref, h2_ref, lg_ref, ext_sc, sh_sc, *, tiles_per_b):
    i = pl.program_id(0)
    ti = i % tiles_per_b
    has_prev = ti > 1
    has_next = jnp.logical_and(ti > 0, ti < tiles_per_b - 1)
    ext_sc[0:CONV_HALO, :] = jnp.where(has_prev, up_ref[...], 0.0)
    ext_sc[CONV_HALO:CONV_HALO + TM, :] = uc_ref[...]
    ext_sc[CONV_HALO + TM:, :] = jnp.where(has_next, un_ref[...], 0.0)
    acc = jnp.zeros((TM, CONV_CH), F32) + dwb_ref[...]
    base = CONV_HALO - CONV_W // 2
    span = TM + (CONV_W // SUBLANES) * SUBLANES
    for r in range(SUBLANES):
        taps = [w for w in range(CONV_W) if (base + w) % SUBLANES == r]
        if not taps:
            continue
        if r == 0:
            src = ext_sc
        else:
            sh_sc[r - 1] = ext_sc[r:r + span, :]
            src = sh_sc.at[r - 1]
        for w in taps:
            a = (base + w) // SUBLANES * SUBLANES
            acc = acc + src[a:a + TM, :] * dww_ref[w:w + 1, :]
    mu = jnp.mean(acc, axis=-1, keepdims=True)
    xc = acc - mu
    var = jnp.mean(xc * xc, axis=-1, keepdims=True)
    ln = xc * lax.rsqrt(var + EPS) * lng_ref[...] + lnb_ref[...]
    act = ln * (1.0 / (1.0 + jnp.exp(-ln)))
    br_a = jnp.dot(act.astype(BF16), pw_ref[...], preferred_element_type=F32)
    br_b = jnp.dot(og_ref[...], wgo_ref[...], preferred_element_type=F32)
    br_c = jnp.dot(od_ref[...], wdo_ref[...], preferred_element_type=F32)
    merged = (gt_ref[:, 0:D].astype(F32) * br_a + gt_ref[:, D:2 * D].astype(F32) * br_b
              + gt_ref[:, 2 * D:3 * D].astype(F32) * br_c)
    y = jnp.dot(merged.astype(BF16), wout_ref[...], preferred_element_type=F32)
    x_new = x_ref[...] + mod_ref[0, 2:3, :] * y
    xo_ref[...] = x_new
    h2 = _modnorm(x_new, g2_ref[...], mod_ref[0, 3:4, :], mod_ref[0, 4:5, :])
    h2_hi = h2.astype(BF16)
    h2_ref[...] = h2_hi
    h2_lo = (h2 - h2_hi.astype(F32)).astype(BF16)
    lg = (jnp.dot(h2_hi, wr_ref[0], preferred_element_type=F32)
          + jnp.dot(h2_lo, wr_ref[0], preferred_element_type=F32)
          + jnp.dot(h2_hi, wr_ref[1], preferred_element_type=F32))
    lg_ref[...] = lg.T[:N_EXP, :] + br_ref[...]


def _mix(u, og, od, gates, x, mod_l, dw_w, dw_b, ln_g, ln_b, pw_w, w_gqa_o, w_diff_o, w_out,
         g2, w_r, b_r, tiles_per_b, n_b):
    T = x.shape[0]
    nt = T // TM
    halo_per_tile = TM // CONV_HALO
    n_halo = T // CONV_HALO
    full = lambda shape: pl.BlockSpec(shape, lambda i: (0,) * len(shape))
    in_specs = [
        pl.BlockSpec((CONV_HALO, CONV_CH), lambda i: (jnp.maximum(i * halo_per_tile - 1, 0), 0)),
        pl.BlockSpec((TM, CONV_CH), lambda i: (i, 0)),
        pl.BlockSpec((CONV_HALO, CONV_CH),
                     lambda i: (jnp.minimum((i + 1) * halo_per_tile, n_halo - 1), 0)),
        pl.BlockSpec((TM, GQA_QH * HD), lambda i: (i, 0)),
        pl.BlockSpec((TM, DIFF_H * DIFF_V), lambda i: (i, 0)),
        pl.BlockSpec((TM, GATE_COLS), lambda i: (i, 0)),
        pl.BlockSpec((TM, D), lambda i: (i, 0)),
        pl.BlockSpec((1, 6, D), lambda i: (_mod_row(i, tiles_per_b, n_b), 0, 0)),
        full((CONV_W + 1, CONV_CH)), full((1, CONV_CH)), full((1, CONV_CH)), full((1, CONV_CH)),
        full((CONV_CH, D)), full((GQA_QH * HD, D)), full((DIFF_H * DIFF_V, D)), full((D, D)),
        full((1, D)), full((2, D, LANES)), full((N_EXP, 1)),
    ]
    out_shape = (jax.ShapeDtypeStruct((T, D), F32), jax.ShapeDtypeStruct((T, D), BF16),
                 jax.ShapeDtypeStruct((N_EXP, T), F32))
    out_specs = (pl.BlockSpec((TM, D), lambda i: (i, 0)), pl.BlockSpec((TM, D), lambda i: (i, 0)),
                 pl.BlockSpec((N_EXP, TM), lambda i: (0, i)))
    return pl.pallas_call(
        functools.partial(_mix_kernel, tiles_per_b=tiles_per_b),
        out_shape=out_shape, grid=(nt,), in_specs=in_specs, out_specs=out_specs,
        scratch_shapes=[pltpu.VMEM((TM + 2 * CONV_HALO, CONV_CH), F32),
                        pltpu.VMEM((SUBLANES - 1, TM + (CONV_W // SUBLANES) * SUBLANES, CONV_CH), F32)],
        compiler_params=_cparams(("arbitrary",)),
    )(u, u, u, og, od, gates, x, mod_l, dw_w, dw_b, ln_g, ln_b, pw_w, w_gqa_o, w_diff_o, w_out,
      g2, w_r, b_r)


def _expert_kernel(be_ref, nb_ref, x_ref, w1_ref, b1_ref, w2_ref, b2_ref, y_ref, w1_sc, w2_sc):
    i = pl.program_id(0)
    prev_e = be_ref[jnp.maximum(i - 1, 0)]
    new_expert = jnp.logical_or(i == 0, be_ref[i] != prev_e)

    @pl.when(new_expert)
    def _():
        w1_sc[...] = w1_ref[0].astype(BF16)
        w2_sc[...] = w2_ref[0].astype(BF16)

    @pl.when(i < nb_ref[0])
    def _():
        x = x_ref[...]
        y = jnp.zeros((EXPERT_BLOCK, D), F32) + b2_ref[0]
        for c0 in range(0, FF, FF_CHUNK):
            c1 = c0 + FF_CHUNK
            g = jnp.dot(x, w1_sc[:, c0:c1], preferred_element_type=F32) + b1_ref[0, :, c0:c1]
            li = (jnp.dot(x, w1_sc[:, FF + c0:FF + c1], preferred_element_type=F32)
                  + b1_ref[0, :, FF + c0:FF + c1])
            gate = jnp.minimum(g, SWIGLU_LIMIT)
            lin = jnp.clip(li, -SWIGLU_LIMIT, SWIGLU_LIMIT)
            act = (lin + 1.0) * (gate * (1.0 / (1.0 + jnp.exp(-gate * SWIGLU_ALPHA))))
            y = y + jnp.dot(act.astype(BF16), w2_sc[c0:c1, :], preferred_element_type=F32)
        y_ref[...] = y.astype(y_ref.dtype)

    @pl.when(i >= nb_ref[0])
    def _():
        y_ref[...] = jnp.zeros(y_ref.shape, y_ref.dtype)


def _experts(xb, blk_e, n_used, w1, b1, w2, b2, l):
    n_slots = xb.shape[0]
    n_blocks = n_slots // EXPERT_BLOCK
    L = w1.shape[0]
    grid_spec = pltpu.PrefetchScalarGridSpec(
        num_scalar_prefetch=2, grid=(n_blocks,),
        in_specs=[
            pl.BlockSpec((EXPERT_BLOCK, D), lambda i, be, nb: (i, 0)),
            pl.BlockSpec((None, 1, D, 2 * FF), lambda i, be, nb: (l, be[i], 0, 0)),
            pl.BlockSpec((None, 1, 1, 2 * FF), lambda i, be, nb: (l, be[i], 0, 0)),
            pl.BlockSpec((None, 1, FF, D), lambda i, be, nb: (l, be[i], 0, 0)),
            pl.BlockSpec((None, 1, 1, D), lambda i, be, nb: (l, be[i], 0, 0)),
        ],
        out_specs=pl.BlockSpec((EXPERT_BLOCK, D), lambda i, be, nb: (i, 0)),
        scratch_shapes=[pltpu.VMEM((D, 2 * FF), BF16), pltpu.VMEM((FF, D), BF16)],
    )
    return pl.pallas_call(
        _expert_kernel, grid_spec=grid_spec,
        out_shape=jax.ShapeDtypeStruct((n_slots, D), BF16),
        compiler_params=_cparams(("arbitrary",)),
    )(blk_e, n_used, xb, w1, b1.reshape(L, N_EXP, 1, 2 * FF), w2, b2.reshape(L, N_EXP, 1, D))


def _combine_kernel(x_ref, yg_ref, cw_ref, mod_ref, o_ref):
    cw = cw_ref[...]
    f = jnp.zeros((TM, D), F32)
    for k in range(TOP_K):
        f = f + cw[:, k:k + 1] * yg_ref[k].astype(F32)
    o_ref[...] = x_ref[...] + mod_ref[0, 5:6, :] * f


def _combine(x, yg, comb, mod_l, tiles_per_b, n_b):
    T = x.shape[0]
    return pl.pallas_call(
        _combine_kernel,
        out_shape=jax.ShapeDtypeStruct((T, D), F32),
        grid=(T // TM,),
        in_specs=[pl.BlockSpec((TM, D), lambda i: (i, 0)),
                  pl.BlockSpec((TOP_K, TM, D), lambda i: (0, i, 0)),
                  pl.BlockSpec((TM, TOP_K), lambda i: (i, 0)),
                  pl.BlockSpec((1, 6, D), lambda i: (_mod_row(i, tiles_per_b, n_b), 0, 0))],
        out_specs=pl.BlockSpec((TM, D), lambda i: (i, 0)),
        compiler_params=_cparams(("arbitrary",)),
    )(x, yg, comb, mod_l)


def _router_kernel(lg_ref, idx_ref, comb_ref, rank_ref, cnt_ref, cnt_sc):
    i = pl.program_id(0)

    @pl.when(i == 0)
    def _():
        cnt_sc[...] = jnp.zeros(cnt_sc.shape, F32)

    x = lg_ref[...]
    sub = lax.broadcasted_iota(jnp.int32, (N_EXP, TM), 0)
    row = lax.broadcasted_iota(jnp.int32, (TOP_K, TM), 0)
    vals, hots = [], []
    idx_out = jnp.zeros((TOP_K, TM), jnp.int32)
    for k in range(TOP_K):
        mx = jnp.max(x, axis=0, keepdims=True)
        idx = jnp.min(jnp.where(x == mx, sub, N_EXP), axis=0, keepdims=True)
        hot = sub == idx
        vals.append(mx)
        hots.append(hot)
        idx_out = jnp.where(row == k, idx, idx_out)
        x = jnp.where(hot, -jnp.inf, x)
    idx_ref[...] = idx_out

    es = [jnp.exp(v - vals[0]) for v in vals]
    denom = es[0] + es[1] + es[2] + es[3]
    comb = jnp.zeros((TOP_K, TM), F32)
    for k in range(TOP_K):
        comb = jnp.where(row == k, es[k] / denom, comb)
    comb_ref[...] = comb

    hot_sum = jnp.zeros((N_EXP, TM), F32)
    for k in range(TOP_K):
        hot_sum = hot_sum + jnp.where(hots[k], 1.0, 0.0)
    r_i = lax.broadcasted_iota(jnp.int32, (TM, TM), 0)
    c_i = lax.broadcasted_iota(jnp.int32, (TM, TM), 1)
    earlier = jnp.where(r_i < c_i, 1.0, 0.0).astype(BF16)
    before = jnp.dot(hot_sum.astype(BF16), earlier, preferred_element_type=F32) + cnt_sc[...]
    rank = jnp.zeros((TOP_K, TM), jnp.int32)
    for k in range(TOP_K):
        rk = jnp.sum(jnp.where(hots[k], before, 0.0), axis=0, keepdims=True)
        rank = jnp.where(row == k, rk.astype(jnp.int32), rank)
    rank_ref[...] = rank
    cnt = cnt_sc[...] + jnp.sum(hot_sum, axis=1, keepdims=True)
    cnt_sc[...] = cnt
    cnt_ref[...] = cnt.astype(jnp.int32)


def _router(logits_t):
    T = logits_t.shape[1]
    tok = lambda r: pl.BlockSpec((r, TM), lambda i: (0, i))
    return pl.pallas_call(
        _router_kernel,
        out_shape=(jax.ShapeDtypeStruct((TOP_K, T), jnp.int32), jax.ShapeDtypeStruct((TOP_K, T), F32),
                   jax.ShapeDtypeStruct((TOP_K, T), jnp.int32), jax.ShapeDtypeStruct((N_EXP, 1), jnp.int32)),
        grid=(T // TM,),
        in_specs=[tok(N_EXP)],
        out_specs=(tok(TOP_K), tok(TOP_K), tok(TOP_K), pl.BlockSpec((N_EXP, 1), lambda i: (0, 0))),
        scratch_shapes=[pltpu.VMEM((N_EXP, 1), F32)],
        compiler_params=_cparams(("arbitrary",)),
    )(logits_t)


def _slot_tables(top_idx, rank, counts):
    t = top_idx.shape[1]
    n_assign = t * TOP_K
    counts = counts.reshape(N_EXP)
    padded = (counts + EXPERT_BLOCK - 1) // EXPERT_BLOCK * EXPERT_BLOCK
    pad_end = jnp.cumsum(padded)
    pad_start = pad_end - padded
    start = jnp.cumsum(counts) - counts
    hot = top_idx[:, :, None] == jnp.arange(N_EXP, dtype=jnp.int32)[None, None, :]
    slot_of = (jnp.sum(jnp.where(hot, pad_start[None, None, :], 0), axis=-1) + rank).reshape(-1)
    n_blocks = -(-n_assign // EXPERT_BLOCK) + N_EXP
    blk_start = jnp.arange(n_blocks, dtype=jnp.int32) * EXPERT_BLOCK
    blk_e = jnp.minimum(jnp.sum((pad_end[None, :] <= blk_start[:, None]).astype(jnp.int32), axis=1),
                        N_EXP - 1).astype(jnp.int32)
    n_used = (pad_end[-1] // EXPERT_BLOCK).astype(jnp.int32).reshape(1)
    tok = jnp.broadcast_to(jnp.arange(t, dtype=jnp.int32)[None, :], (TOP_K, t)).reshape(-1)
    _, tok_sorted = lax.sort_key_val(slot_of, tok)
    hot_b = blk_e[:, None] == jnp.arange(N_EXP, dtype=jnp.int32)[None, :]
    sel = lambda v: jnp.sum(jnp.where(hot_b, v[None, :], 0), axis=1)
    r0 = blk_start - sel(pad_start)
    r = r0[:, None] + jnp.arange(EXPERT_BLOCK, dtype=jnp.int32)[None, :]
    valid = r < sel(counts)[:, None]
    src = jnp.where(valid, sel(start)[:, None] + r, 0)
    slot_tok = jnp.where(valid, tok_sorted[src.reshape(-1)].reshape(src.shape), 0).reshape(-1)
    return slot_tok, slot_of.astype(jnp.int32), blk_e, n_used


def _rope_tables(tiles_per_b):
    n_lat = (tiles_per_b - 1) * TM
    pos = jnp.arange(n_lat, dtype=F32)
    t_row = jnp.floor(pos / GRID_W)
    t_col = pos - t_row * GRID_W
    n_freq = HD // 4
    inv_freq = ROPE_THETA ** (-jnp.arange(n_freq, dtype=F32) / n_freq)
    lane = jnp.arange(LANES)
    d = lane % HD
    use_col = (d // 32) == 1
    f = d % n_freq
    first = (d % 32) < n_freq
    ang = jnp.where(use_col[None, :], t_col[:, None], t_row[:, None]) * inv_freq[f][None, :]
    cos = jnp.cos(ang)
    sin = jnp.where(first[None, :], -jnp.sin(ang), jnp.sin(ang))
    cos = jnp.concatenate([jnp.ones((TM, LANES), F32), cos], axis=0)
    sin = jnp.concatenate([jnp.zeros((TM, LANES), F32), sin], axis=0)
    return cos, sin


def _permute_w_in(w):
    gk, gv, dk, dv = w[:, 0:128], w[:, 128:256], w[:, 256:768], w[:, 768:1280]
    return jnp.concatenate([gk, dk, w[:, 1280:2304], gv, dv, w[:, 2304:]], axis=1).astype(BF16)


def kernel(x, c, ctx, c_ctx, w_mod, b_mod, norm1_g, norm2_g, w_in, gqa_q_norm, gqa_k_norm, diff_q_norm, diff_k_norm, lam_q1, lam_k1, lam_q2, lam_k2, diff_subln_g, conv_dw_w, conv_dw_b, conv_ln_g, conv_ln_b, conv_pw_w, w_gqa_o, w_diff_o, b_gate, w_out, router_w, router_b, exp_w1, exp_b1, exp_w2, exp_b2):
    n_b, n_lat, d = x.shape
    n_ctx = ctx.shape[1]
    L = w_mod.shape[0]
    assert d == D and n_ctx == TM and n_lat % TM == 0 and n_b < 8
    tiles_per_b = (n_ctx + n_lat) // TM
    assert (tiles_per_b - 1) % UNROLL == 0
    T = n_b * tiles_per_b * TM

    n_streams = N_STREAMS if n_b % N_STREAMS == 0 else 1
    nb_s = n_b // n_streams
    T_s = T // n_streams
    xs_all = jnp.concatenate([ctx, x], axis=1).reshape(n_streams, T_s, D)
    streams = [xs_all[s] for s in range(n_streams)]
    cc = jnp.zeros((n_streams, 8, D), F32)
    for s in range(n_streams):
        cc = cc.at[s, :nb_s].set(c[s * nb_s:(s + 1) * nb_s]).at[s, nb_s].set(c_ctx)
    mod = _modulation(cc.reshape(n_streams * 8, D), w_mod, b_mod).reshape(L, n_streams, 8, 6, D)
    cos_t, sin_t = _rope_tables(tiles_per_b)
    q_scale = HD ** -0.5 * math.log2(math.e)

    for l in range(L):
        lam_init = 0.8 - 0.6 * math.exp(-0.3 * l)
        tile2 = lambda g: jnp.tile(g, 2)
        qk_gain = jnp.concatenate(
            [tile2(gqa_k_norm[l])] * (GQA_KVH // 2) + [tile2(diff_k_norm[l])] * DIFF_H
            + [tile2(gqa_q_norm[l]) * q_scale] * (GQA_QH // 2)
            + [tile2(diff_q_norm[l]) * q_scale] * DIFF_H).reshape(1, QK_COLS)
        w_in_l = _permute_w_in(w_in[l])
        dw_w = jnp.concatenate([conv_dw_w[l], jnp.zeros((1, CONV_CH), F32)], axis=0)
        w_r_pad = jnp.pad(router_w[l], ((0, 0), (0, LANES - N_EXP)))
        w_r_hi = w_r_pad.astype(BF16)
        w_r = jnp.stack([w_r_hi, (w_r_pad - w_r_hi.astype(F32)).astype(BF16)])
        mix_w = (dw_w, conv_dw_b[l].reshape(1, CONV_CH),
                 conv_ln_g[l].reshape(1, CONV_CH), conv_ln_b[l].reshape(1, CONV_CH),
                 conv_pw_w[l].astype(BF16), w_gqa_o[l].astype(BF16), w_diff_o[l].astype(BF16),
                 w_out[l].astype(BF16), norm2_g[l].reshape(1, D), w_r, router_b[l].reshape(N_EXP, 1))
        lam_refs = [v[l].reshape(1, HD) for v in (lam_q1, lam_k1, lam_q2, lam_k2)]
        for s in range(n_streams):
            xs = streams[s]
            mod_l = mod[l, s]
            kg, kd, qt, vtg, vtd, u, gates = _inproj(
                xs, mod_l, norm1_g[l].reshape(1, D), w_in_l, cos_t, sin_t, qk_gain,
                b_gate[l].reshape(1, GATE_COLS), tiles_per_b, nb_s)
            og = _attention(qt, kg, vtg, nb_s, tiles_per_b, diff=False)
            od = _attention(qt, kd, vtd, nb_s, tiles_per_b, diff=True, lam_refs=lam_refs,
                            subln_g=diff_subln_g[l].reshape(1, DIFF_V), lam_init=lam_init)
            xs, h2, logits_t = _mix(u, og, od, gates, xs, mod_l, *mix_w, tiles_per_b, nb_s)
            top_idx, comb, rank, counts = _router(logits_t)
            slot_tok, slot_of, blk_e, n_used = _slot_tables(top_idx, rank, counts)
            xb = h2[slot_tok]
            yb = _experts(xb, blk_e, n_used, exp_w1, exp_b1, exp_w2, exp_b2, l)
            yg = yb[slot_of].reshape(TOP_K, T_s, D)
            streams[s] = _combine(xs, yg, comb.T, mod_l, tiles_per_b, nb_s)

    out = jnp.stack(streams).reshape(n_b, n_ctx + n_lat, D)
    return out[:, n_ctx:, :]
```
